```python
import math
import jax, jax.numpy as jnp
from jax import lax
import numpy as np

D_MODEL = 2048
BATCH = 16
SEQ = 2048
DEPTH = 2

N_MIXERS = 2
N_MOBA_LAYERS = (DEPTH + 1) // 2
N_GDN_LAYERS = DEPTH // 2
DEEPNORM_ALPHA = float((2 * DEPTH) ** 0.25)
DEEPNORM_BETA = float((8 * DEPTH) ** -0.25)
LN_EPS = 1e-5
ADA_SCALE = 0.2
MOBA_HEADS = 16
MOBA_HEAD_DIM = D_MODEL // MOBA_HEADS
MOBA_DIM = MOBA_HEADS * MOBA_HEAD_DIM
MOBA_BLOCK = 256
MOBA_TOPK = 3
MOBA_QCHUNK = 128
ROPE_THETA = 500000.0
ROPE_DIM = MOBA_HEAD_DIM // 4
MAX_POS_OFFSET = 4096
NEG_INF = -1e30
GDN_K_HEADS = 16
GDN_V_HEADS = 32
GDN_HEAD_DIM = 128
GDN_KEY_DIM = GDN_K_HEADS * GDN_HEAD_DIM
GDN_VALUE_DIM = GDN_V_HEADS * GDN_HEAD_DIM
GDN_CONV_DIM = 2 * GDN_KEY_DIM + GDN_VALUE_DIM
GDN_IN_DIM = GDN_CONV_DIM + GDN_VALUE_DIM + 2 * GDN_V_HEADS
GDN_CONV = 4
GDN_CHUNK = 64
GDN_NORM_EPS = 1e-6
L2_EPS = 1e-6
N_EXPERTS = 32
MOE_TOPK = 4
D_EXPERT = D_MODEL
SWIGLU_ALPHA = 1.702
SWIGLU_LIMIT = 7.0
MOE_BLOCK = 256

kernel_name = "hybrid_moba_gdn_moe_deepnorm_adaln"


def layer_norm(x, g, b):
    xf = x.astype(jnp.float32)
    mu = xf.mean(-1, keepdims=True)
    var = jnp.square(xf - mu).mean(-1, keepdims=True)
    return ((xf - mu) * lax.rsqrt(var + LN_EPS) * g + b).astype(x.dtype)


def adaln_modulation(c, w, b):
    m = jax.nn.silu(c) @ w + b
    shift, scale, gate = jnp.split(m[:, None, :], 3, axis=-1)
    return shift, scale, gate


def deepnorm_residual(x, y, gate, g, b):
    return layer_norm(DEEPNORM_ALPHA * x + (1.0 + gate) * y, g, b)


def rotary_tables(positions):
    inv_freq = ROPE_THETA ** (-jnp.arange(0, ROPE_DIM, 2, dtype=jnp.float32) / ROPE_DIM)
    ang = positions.astype(jnp.float32)[..., None] * inv_freq
    return jnp.cos(ang)[:, :, None, :], jnp.sin(ang)[:, :, None, :]


def apply_partial_rope(t, cos, sin):
    tr = t[..., :ROPE_DIM].astype(jnp.float32)
    t1, t2 = tr[..., :ROPE_DIM // 2], tr[..., ROPE_DIM // 2:]
    rot = jnp.concatenate([t1 * cos - t2 * sin, t2 * cos + t1 * sin], axis=-1)
    return jnp.concatenate([rot.astype(t.dtype), t[..., ROPE_DIM:]], axis=-1)


def group_rows(group_ids, n_groups, block, n_slots):
    n_items = group_ids.shape[0]
    counts = jax.ops.segment_sum(jnp.ones_like(group_ids), group_ids, num_segments=n_groups + 1)[:n_groups]
    padded = (counts + block - 1) // block * block
    pad_end = jnp.cumsum(padded)
    pad_start = pad_end - padded
    start = jnp.cumsum(counts) - counts
    order = jnp.argsort(group_ids, stable=True).astype(jnp.int32)
    g_sorted = group_ids[order]
    g_safe = jnp.minimum(g_sorted, n_groups - 1)
    dest = jnp.where(g_sorted < n_groups,
                     pad_start[g_safe] + jnp.arange(n_items, dtype=jnp.int32) - start[g_safe], n_slots)
    slot_item = jnp.full((n_slots,), n_items, jnp.int32).at[dest].set(order, mode='drop')
    block_group = jnp.minimum(
        jnp.searchsorted(pad_end, jnp.arange(n_slots // block, dtype=jnp.int32) * block, side='right'),
        n_groups - 1).astype(jnp.int32)
    return slot_item, block_group


_row_gather = jax.vmap(lambda arr, r: arr[r])
_row_set = jax.vmap(lambda arr, r, val: arr.at[r].set(val))


def moba_attention(h, cos, sin, w_qkv, w_o):
    Bn, S, _ = h.shape
    BH = Bn * MOBA_HEADS
    HD = MOBA_HEAD_DIM
    q, k, v = jnp.split((h @ w_qkv).reshape(Bn, S, 3 * MOBA_HEADS, HD), 3, axis=2)
    q = apply_partial_rope(q, cos, sin)
    k = apply_partial_rope(k, cos, sin)

    def heads_major(t):
        return t.transpose(0, 2, 1, 3).reshape(BH, S, HD).astype(jnp.float32)

    q, k, v = heads_major(q), heads_major(k), heads_major(v)
    n_blk = -(-S // MOBA_BLOCK)
    s_pad = n_blk * MOBA_BLOCK
    k_sel = min(MOBA_TOPK, n_blk - 1)
    pad = ((0, 0), (0, s_pad - S), (0, 0))
    qb = jnp.pad(q, pad).reshape(BH, n_blk, MOBA_BLOCK, HD)
    kb = jnp.pad(k, pad).reshape(BH, n_blk, MOBA_BLOCK, HD)
    vb = jnp.pad(v, pad).reshape(BH, n_blk, MOBA_BLOCK, HD)
    scale = HD ** -0.5
    causal = jnp.tril(jnp.ones((MOBA_BLOCK, MOBA_BLOCK), bool))

    def own_block(args):
        qj, kj, vj = args
        s = jnp.where(causal, jnp.einsum('nqd,nkd->nqk', qj, kj) * scale, NEG_INF)
        m = s.max(-1)
        p = jnp.exp(s - m[..., None])
        return m, p.sum(-1), jnp.einsum('nqk,nkd->nqd', p, vj)

    m0, l0, o0 = lax.map(own_block, (qb.swapaxes(0, 1), kb.swapaxes(0, 1), vb.swapaxes(0, 1)))
    m_run = m0.swapaxes(0, 1).reshape(BH, s_pad)[:, :S]
    l_run = l0.swapaxes(0, 1).reshape(BH, s_pad)[:, :S]
    o_run = o0.swapaxes(0, 1).reshape(BH, s_pad, HD)[:, :S]

    if k_sel > 0:
        k_mean = kb.mean(axis=2)
        q_blk = jnp.arange(S, dtype=jnp.int32) // MOBA_BLOCK
        past = jnp.arange(n_blk, dtype=jnp.int32)[None, :] < q_blk[:, None]
        gate = jnp.where(past, jnp.einsum('nsd,njd->nsj', q, k_mean), NEG_INF)
        _, sel = lax.top_k(gate, k_sel)
        gid = jnp.where(sel < q_blk[None, :, None], sel, n_blk).astype(jnp.int32)
        gid = gid.reshape(BH, S * k_sel)
        n_slots = S * k_sel + n_blk * MOBA_QCHUNK
        n_chunks = n_slots // MOBA_QCHUNK
        slot_item, chunk_blk = jax.vmap(lambda g_: group_rows(g_, n_blk, MOBA_QCHUNK, n_slots))(gid)
        slot_q = (slot_item // k_sel).reshape(BH, n_chunks, MOBA_QCHUNK).swapaxes(0, 1)
        q_ext = jnp.concatenate([q, jnp.zeros((BH, 1, HD), jnp.float32)], axis=1)
        carry0 = (jnp.concatenate([m_run, jnp.zeros((BH, 1), jnp.float32)], axis=1),
                  jnp.concatenate([l_run, jnp.ones((BH, 1), jnp.float32)], axis=1),
                  jnp.concatenate([o_run, jnp.zeros((BH, 1, HD), jnp.float32)], axis=1))
        bh = jnp.arange(BH)

        def step(carry, inp):
            m_c, l_c, o_c = carry
            rows, blk = inp
            qc = _row_gather(q_ext, rows)
            kc = kb[bh, blk]
            vc = vb[bh, blk]
            s = jnp.einsum('nqd,nkd->nqk', qc, kc) * scale
            m_b = s.max(-1)
            p = jnp.exp(s - m_b[..., None])
            l_b = p.sum(-1)
            o_b = jnp.einsum('nqk,nkd->nqd', p, vc)
            m_old = _row_gather(m_c, rows)
            l_old = _row_gather(l_c, rows)
            o_old = _row_gather(o_c, rows)
            m_new = jnp.maximum(m_old, m_b)
            a_old = jnp.exp(m_old - m_new)
            a_b = jnp.exp(m_b - m_new)
            m_c = _row_set(m_c, rows, m_new)
            l_c = _row_set(l_c, rows, l_old * a_old + l_b * a_b)
            o_c = _row_set(o_c, rows, o_old * a_old[..., None] + o_b * a_b[..., None])
            return (m_c, l_c, o_c), None

        (_, l_ext, o_ext), _ = lax.scan(step, carry0, (slot_q, chunk_blk.swapaxes(0, 1)))
        l_run = l_ext[:, :S]
        o_run = o_ext[:, :S]

    out = o_run / l_run[..., None]
    out = out.reshape(Bn, MOBA_HEADS, S, HD).transpose(0, 2, 1, 3).reshape(Bn, S, MOBA_DIM)
    return out.astype(h.dtype) @ w_o


def causal_depthwise_conv(t, w):
    return lax.conv_general_dilated(t, w[:, None, :].astype(t.dtype), window_strides=(1,),
                                    padding=[(GDN_CONV - 1, 0)],
                                    dimension_numbers=('NWC', 'WIO', 'NWC'),
                                    feature_group_count=t.shape[-1])


def l2_normalize(t):
    tf = t.astype(jnp.float32)
    return tf * lax.rsqrt(jnp.sum(tf * tf, axis=-1, keepdims=True) + L2_EPS)


def gated_rms_norm(o, z, w):
    of = o.astype(jnp.float32)
    of = of * lax.rsqrt(jnp.mean(of * of, axis=-1, keepdims=True) + GDN_NORM_EPS)
    return of * w * jax.nn.silu(z.astype(jnp.float32))


def chunk_gated_delta_rule(q, k, v, g, beta):
    Bn, S, H, DK = k.shape
    DV = v.shape[-1]
    C = GDN_CHUNK
    pad = (-S) % C
    n = (S + pad) // C

    def prep(t):
        t = jnp.moveaxis(t.astype(jnp.float32), 1, 2)
        widths = [(0, 0)] * t.ndim
        widths[2] = (0, pad)
        return jnp.pad(t, widths)

    q, k, v, g, beta = prep(q) * DK ** -0.5, prep(k), prep(v), prep(g), prep(beta)
    k_beta = (k * beta[..., None]).reshape(Bn, H, n, C, DK)
    v_beta = (v * beta[..., None]).reshape(Bn, H, n, C, DV)
    q = q.reshape(Bn, H, n, C, DK)
    k = k.reshape(Bn, H, n, C, DK)
    g = jnp.cumsum(g.reshape(Bn, H, n, C), axis=-1)
    tril = jnp.tril(jnp.ones((C, C), bool))
    strict = jnp.tril(jnp.ones((C, C), bool), -1)
    decay = jnp.exp(jnp.where(tril, g[..., :, None] - g[..., None, :], NEG_INF))
    m = jnp.where(strict, jnp.einsum('bhncd,bhnsd->bhncs', k_beta, k) * decay, 0.0)
    rhs = jnp.concatenate([v_beta, k_beta * jnp.exp(g)[..., None]], axis=-1)
    sol = lax.linalg.triangular_solve(m, rhs, left_side=True, lower=True, unit_diagonal=True)
    u, w = sol[..., :DV], sol[..., DV:]
    attn = jnp.where(tril, jnp.einsum('bhncd,bhnsd->bhncs', q, k) * decay, 0.0)
    xs = tuple(jnp.moveaxis(t, 2, 0) for t in (q, k, u, w, g, attn))

    def step(state, inp):
        q_i, k_i, u_i, w_i, g_i, a_i = inp
        v_new = u_i - jnp.einsum('bhck,bhkv->bhcv', w_i, state)
        o = (jnp.einsum('bhck,bhkv->bhcv', q_i * jnp.exp(g_i)[..., None], state)
             + jnp.einsum('bhcs,bhsv->bhcv', a_i, v_new))
        g_last = g_i[..., -1]
        state = (state * jnp.exp(g_last)[..., None, None]
                 + jnp.einsum('bhck,bhcv->bhkv', k_i * jnp.exp(g_last[..., None] - g_i)[..., None], v_new))
        return state, o

    _, o = lax.scan(step, jnp.zeros((Bn, H, DK, DV), jnp.float32), xs)
    o = jnp.moveaxis(o, 0, 2).reshape(Bn, H, n * C, DV)[:, :, :S]
    return jnp.moveaxis(o, 1, 2)


def gated_deltanet(h, w_in, conv_w, a_log, dt_bias, norm_w, w_out):
    Bn, S, _ = h.shape
    HD = GDN_HEAD_DIM
    proj = h @ w_in
    qkv = jax.nn.silu(causal_depthwise_conv(proj[..., :GDN_CONV_DIM], conv_w))
    z = proj[..., GDN_CONV_DIM:GDN_CONV_DIM + GDN_VALUE_DIM]
    b = proj[..., GDN_CONV_DIM + GDN_VALUE_DIM:GDN_CONV_DIM + GDN_VALUE_DIM + GDN_V_HEADS]
    a = proj[..., GDN_CONV_DIM + GDN_VALUE_DIM + GDN_V_HEADS:]
    q = qkv[..., :GDN_KEY_DIM].reshape(Bn, S, GDN_K_HEADS, HD)
    k = qkv[..., GDN_KEY_DIM:2 * GDN_KEY_DIM].reshape(Bn, S, GDN_K_HEADS, HD)
    v = qkv[..., 2 * GDN_KEY_DIM:].reshape(Bn, S, GDN_V_HEADS, HD)
    rep = GDN_V_HEADS // GDN_K_HEADS
    q = jnp.repeat(l2_normalize(q), rep, axis=2)
    k = jnp.repeat(l2_normalize(k), rep, axis=2)
    beta = jax.nn.sigmoid(b.astype(jnp.float32))
    g = -jnp.exp(a_log.astype(jnp.float32)) * jax.nn.softplus(a.astype(jnp.float32) + dt_bias)
    o = chunk_gated_delta_rule(q, k, v, g, beta)
    o = gated_rms_norm(o, z.reshape(Bn, S, GDN_V_HEADS, HD), norm_w)
    return o.reshape(Bn, S, GDN_VALUE_DIM).astype(h.dtype) @ w_out


def clamped_swiglu(hu):
    gate, up = jnp.split(hu, 2, axis=-1)
    gate = jnp.minimum(gate, SWIGLU_LIMIT)
    up = jnp.clip(up, -SWIGLU_LIMIT, SWIGLU_LIMIT)
    return gate * jax.nn.sigmoid(SWIGLU_ALPHA * gate) * (up + 1.0)


def moe_ffn(h, w_router, b_router, w_gate_up, b_gate_up, w_down, b_down):
    Bn, S, D = h.shape
    T = Bn * S
    xt = h.reshape(T, D)
    logits = (xt @ w_router + b_router).astype(jnp.float32)
    top_val, top_idx = lax.top_k(logits, MOE_TOPK)
    gates = jax.nn.softmax(top_val, axis=-1)
    n_items = T * MOE_TOPK
    n_slots = n_items + N_EXPERTS * MOE_BLOCK
    slot_item, block_exp = group_rows(top_idx.reshape(n_items).astype(jnp.int32), N_EXPERTS, MOE_BLOCK, n_slots)
    slot_tok = (slot_item // MOE_TOPK).reshape(-1, MOE_BLOCK)
    slot_gate = jnp.concatenate([gates.reshape(n_items), jnp.zeros((1,), jnp.float32)])[slot_item]
    slot_gate = slot_gate.reshape(-1, MOE_BLOCK)
    x_ext = jnp.concatenate([xt, jnp.zeros((1, D), xt.dtype)], axis=0)

    def expert_block(acc, inp):
        tok, gate, e = inp
        hu = x_ext[tok] @ w_gate_up[e] + b_gate_up[e]
        y = clamped_swiglu(hu.astype(jnp.float32)).astype(h.dtype) @ w_down[e] + b_down[e]
        return acc.at[tok].add(y.astype(jnp.float32) * gate[:, None]), None

    acc, _ = lax.scan(expert_block, jnp.zeros((T + 1, D), jnp.float32), (slot_tok, slot_gate, block_exp))
    return acc[:T].reshape(Bn, S, D).astype(h.dtype)


def setup_inputs(seed: int = 0) -> dict:
    key = jax.random.key(seed)
    ks = jax.random.split(key, 24)
    D = D_MODEL
    f32 = jnp.float32

    def nrm(k_, shape, s):
        return jax.random.normal(k_, shape, f32) * s

    x = nrm(ks[0], (BATCH, SEQ, D), 1.0)
    c = nrm(ks[1], (BATCH, D), 1.0)
    positions = (jax.random.randint(ks[2], (BATCH, 1), 0, MAX_POS_OFFSET, jnp.int32)
                 + jnp.arange(SEQ, dtype=jnp.int32)[None, :])
    ada_w = nrm(ks[3], (DEPTH, 2, D, 3 * D), ADA_SCALE * D ** -0.5)
    ada_b = nrm(ks[4], (DEPTH, 2, 3 * D), 0.02)
    ln_g = 1.0 + nrm(ks[5], (DEPTH, 2, D), 0.02)
    ln_b = nrm(ks[6], (DEPTH, 2, D), 0.02)
    qkv_col = jnp.concatenate([jnp.ones((2 * MOBA_DIM,), f32), jnp.full((MOBA_DIM,), DEEPNORM_BETA, f32)])
    moba_w_qkv = nrm(ks[7], (N_MOBA_LAYERS, D, 3 * MOBA_DIM), D ** -0.5) * qkv_col
    moba_w_o = nrm(ks[8], (N_MOBA_LAYERS, MOBA_DIM, D), DEEPNORM_BETA * MOBA_DIM ** -0.5)
    in_col = jnp.concatenate([jnp.ones((2 * GDN_KEY_DIM,), f32),
                              jnp.full((GDN_VALUE_DIM,), DEEPNORM_BETA, f32),
                              jnp.ones((GDN_VALUE_DIM + 2 * GDN_V_HEADS,), f32)])
    gdn_w_in = nrm(ks[9], (N_GDN_LAYERS, D, GDN_IN_DIM), D ** -0.5) * in_col
    gdn_conv_w = nrm(ks[10], (N_GDN_LAYERS, GDN_CONV, GDN_CONV_DIM), GDN_CONV ** -0.5)
    gdn_a_log = jnp.log(jax.random.uniform(ks[11], (N_GDN_LAYERS, GDN_V_HEADS), f32, 1.0, 16.0))
    dt = jnp.exp(jax.random.uniform(ks[12], (N_GDN_LAYERS, GDN_V_HEADS), f32,
                                    math.log(1e-3), math.log(1e-1)))
    gdn_dt_bias = dt + jnp.log(-jnp.expm1(-dt))
    gdn_norm_w = 1.0 + nrm(ks[13], (N_GDN_LAYERS, GDN_HEAD_DIM), 0.02)
    gdn_w_out = nrm(ks[14], (N_GDN_LAYERS, GDN_VALUE_DIM, D), DEEPNORM_BETA * GDN_VALUE_DIM ** -0.5)
    router_w = nrm(ks[15], (DEPTH, D, N_EXPERTS), D ** -0.5)
    router_b = nrm(ks[16], (DEPTH, N_EXPERTS), 0.01)
    moe_w_gate_up = nrm(ks[17], (DEPTH, N_EXPERTS, D, 2 * D_EXPERT), D ** -0.5)
    moe_b_gate_up = nrm(ks[18], (DEPTH, N_EXPERTS, 2 * D_EXPERT), 0.01)
    moe_w_down = nrm(ks[19], (DEPTH, N_EXPERTS, D_EXPERT, D), DEEPNORM_BETA * D_EXPERT ** -0.5)
    moe_b_down = nrm(ks[20], (DEPTH, N_EXPERTS, D), 0.01)
    return {'x': x, 'c': c, 'positions': positions, 'ada_w': ada_w, 'ada_b': ada_b,
            'ln_g': ln_g, 'ln_b': ln_b, 'moba_w_qkv': moba_w_qkv, 'moba_w_o': moba_w_o,
            'gdn_w_in': gdn_w_in, 'gdn_conv_w': gdn_conv_w, 'gdn_a_log': gdn_a_log,
            'gdn_dt_bias': gdn_dt_bias, 'gdn_norm_w': gdn_norm_w, 'gdn_w_out': gdn_w_out,
            'router_w': router_w, 'router_b': router_b, 'moe_w_gate_up': moe_w_gate_up,
            'moe_b_gate_up': moe_b_gate_up, 'moe_w_down': moe_w_down, 'moe_b_down': moe_b_down}


def reference(x, c, positions, ada_w, ada_b, ln_g, ln_b, moba_w_qkv, moba_w_o,
              gdn_w_in, gdn_conv_w, gdn_a_log, gdn_dt_bias, gdn_norm_w, gdn_w_out,
              router_w, router_b, moe_w_gate_up, moe_b_gate_up, moe_w_down, moe_b_down):
    cos, sin = rotary_tables(positions)
    for i in range(DEPTH):
        j = i // N_MIXERS
        shift, scale, gate = adaln_modulation(c, ada_w[i, 0], ada_b[i, 0])
        h = x * (1.0 + scale) + shift
        if i % N_MIXERS == 0:
            y = moba_attention(h, cos, sin, moba_w_qkv[j], moba_w_o[j])
        else:
            y = gated_deltanet(h, gdn_w_in[j], gdn_conv_w[j], gdn_a_log[j], gdn_dt_bias[j],
                               gdn_norm_w[j], gdn_w_out[j])
        x = deepnorm_residual(x, y, gate, ln_g[i, 0], ln_b[i, 0])
        shift, scale, gate = adaln_modulation(c, ada_w[i, 1], ada_b[i, 1])
        h = x * (1.0 + scale) + shift
        y = moe_ffn(h, router_w[i], router_b[i], moe_w_gate_up[i], moe_b_gate_up[i],
                    moe_w_down[i], moe_b_down[i])
        x = deepnorm_residual(x, y, gate, ln_g[i, 1], ln_b[i, 1])
    return x
```

```python
import functools
import math

import jax
import jax.numpy as jnp
from jax import lax
from jax.experimental import pallas as pl
from jax.experimental.pallas import tpu as pltpu

F32 = jnp.float32
BF16 = jnp.bfloat16

D_MODEL = 2048
DEPTH = 2
DEEPNORM_ALPHA = float((2 * DEPTH) ** 0.25)
LN_EPS = 1e-5
MOBA_HEADS = 16
HEAD_DIM = 128
MOBA_BLOCK = 256
MOBA_TOPK = 3
ROPE_THETA = 500000.0
ROPE_DIM = HEAD_DIM // 4
NEG_INF = -1e30
GDN_K_HEADS = 16
GDN_V_HEADS = 32
GDN_KEY_DIM = GDN_K_HEADS * HEAD_DIM
GDN_VALUE_DIM = GDN_V_HEADS * HEAD_DIM
GDN_CONV_DIM = 2 * GDN_KEY_DIM + GDN_VALUE_DIM
GDN_MAIN_DIM = GDN_CONV_DIM + GDN_VALUE_DIM
GDN_CONV = 4
GDN_CHUNK = 128
GDN_NORM_EPS = 1e-6
L2_EPS = 1e-6
N_EXPERTS = 32
MOE_TOPK = 4
SWIGLU_ALPHA = 1.702
SWIGLU_LIMIT = 7.0

LANES = 128
VMEM_LIMIT = 50 * 1024 * 1024

MOE_TM = 512
MOE_TN = 1024
GATHER_ROWS = 1024
POST_TM = 256


def _cparams(sem):
    return pltpu.CompilerParams(dimension_semantics=sem, vmem_limit_bytes=VMEM_LIMIT)


def _mm_kernel(*refs, silu_a, has_bias):
    if has_bias:
        a_ref, b_ref, bias_ref, o_ref = refs
    else:
        a_ref, b_ref, o_ref = refs
    a = a_ref[...]
    if silu_a:
        a = a.astype(F32)
        a = a * jax.nn.sigmoid(a)
    acc = jnp.dot(a.astype(BF16), b_ref[...], preferred_element_type=F32)
    if has_bias:
        acc = acc + bias_ref[...]
    o_ref[...] = acc.astype(o_ref.dtype)


def _matmul(a, b, *, out_dtype, tm, tn, bias=None, silu_a=False, name="mm"):
    m, k = a.shape
    n = b.shape[1]
    tm = min(tm, m)
    tn = min(tn, n)
    assert m % tm == 0 and n % tn == 0, (m, n, tm, tn)
    in_specs = [pl.BlockSpec((tm, k), lambda i, j: (i, 0)),
                pl.BlockSpec((k, tn), lambda i, j: (0, j))]
    args = [a, b]
    if bias is not None:
        in_specs.append(pl.BlockSpec((1, tn), lambda i, j: (0, j)))
        args.append(bias.reshape(1, n).astype(F32))
    return pl.pallas_call(
        functools.partial(_mm_kernel, silu_a=silu_a, has_bias=bias is not None),
        grid=(m // tm, n // tn),
        in_specs=in_specs,
        out_specs=pl.BlockSpec((tm, tn), lambda i, j: (i, j)),
        out_shape=jax.ShapeDtypeStruct((m, n), out_dtype),
        compiler_params=_cparams(("parallel", "parallel")),
        name=name,
    )(*args)


def _modulate_kernel(x_ref, scale_ref, shift_ref, h_ref):
    h = x_ref[...] * (1.0 + scale_ref[0]) + shift_ref[0]
    h_ref[...] = h.astype(h_ref.dtype)


def _modulate(x, scale, shift, seq, out_dtype):
    t, d = x.shape
    tm = POST_TM
    per_b = seq // tm
    vec = pl.BlockSpec((1, 1, d), lambda i: (i // per_b, 0, 0))
    return pl.pallas_call(
        _modulate_kernel,
        grid=(t // tm,),
        in_specs=[pl.BlockSpec((tm, d), lambda i: (i, 0)), vec, vec],
        out_specs=pl.BlockSpec((tm, d), lambda i: (i, 0)),
        out_shape=jax.ShapeDtypeStruct((t, d), out_dtype),
        compiler_params=_cparams(("parallel",)),
        name="modulate",
    )(x, scale[:, None, :], shift[:, None, :])


def _top4(logits):
    lane = lax.broadcasted_iota(jnp.int32, logits.shape, 1).astype(F32)
    cur = logits
    vals, idxs = [], []
    for _ in range(MOE_TOPK):
        m = jnp.max(cur, axis=1, keepdims=True)
        idx = jnp.min(jnp.where(cur == m, lane, float(LANES)), axis=1, keepdims=True)
        vals.append(m)
        idxs.append(idx)
        cur = jnp.where(lane == idx, -3.0e38, cur)
    exps = [jnp.exp(v - vals[0]) for v in vals]
    denom = exps[0] + exps[1] + exps[2] + exps[3]
    idx_out = jnp.zeros_like(logits)
    gate_out = jnp.zeros_like(logits)
    for k in range(MOE_TOPK):
        idx_out = jnp.where(lane == float(k), idxs[k], idx_out)
        gate_out = jnp.where(lane == float(k), exps[k] / denom, gate_out)
    return idx_out.astype(jnp.int32), gate_out


def _post_kernel(*refs, moe_in, has_next, has_router, d):
    it = iter(refs)
    x_ref = next(it)
    y_ref = next(it)
    w4_ref = next(it) if moe_in else None
    gate_ref = next(it)
    lng_ref = next(it)
    lnb_ref = next(it)
    if has_next:
        scale_ref = next(it)
        shift_ref = next(it)
    if has_router:
        wr_ref = next(it)
        br_ref = next(it)
    xo_ref = next(it)
    if has_next:
        h_ref = next(it)
    if has_router:
        idx_ref = next(it)
        gw_ref = next(it)

    if moe_in:
        w4 = w4_ref[...]
        y = w4[:, 0:1] * y_ref[:, 0:d]
        for k in range(1, MOE_TOPK):
            y = y + w4[:, k:k + 1] * y_ref[:, k * d:(k + 1) * d]
    else:
        y = y_ref[...].astype(F32)
    z = DEEPNORM_ALPHA * x_ref[...] + (1.0 + gate_ref[0]) * y
    mu = jnp.mean(z, axis=1, keepdims=True)
    zc = z - mu
    var = jnp.mean(zc * zc, axis=1, keepdims=True)
    xn = zc * lax.rsqrt(var + LN_EPS) * lng_ref[...] + lnb_ref[...]
    xo_ref[...] = xn
    if has_next:
        h = xn * (1.0 + scale_ref[0]) + shift_ref[0]
        h_ref[...] = h.astype(h_ref.dtype)
    if has_router:
        logits = jnp.dot(h, wr_ref[...], precision=lax.Precision.HIGHEST,
                         preferred_element_type=F32) + br_ref[...]
        idx, gw = _top4(logits)
        idx_ref[...] = idx
        gw_ref[...] = gw


def _post(x, y, gate, lng, lnb, seq, *, w4=None, nxt=None, router=None, h_dtype=BF16):
    t, d = x.shape
    tm = POST_TM
    per_b = seq // tm
    row = pl.BlockSpec((tm, d), lambda i: (i, 0))
    vec = pl.BlockSpec((1, 1, d), lambda i: (i // per_b, 0, 0))
    full = lambda a: pl.BlockSpec(a.shape, lambda i: (0,) * a.ndim)
    moe_in = w4 is not None
    args = [x, y]
    in_specs = [row, pl.BlockSpec((tm, y.shape[1]), lambda i: (i, 0))]
    if moe_in:
        args.append(w4)
        in_specs.append(pl.BlockSpec((tm, LANES), lambda i: (i, 0)))
    lng2, lnb2 = lng.reshape(1, d), lnb.reshape(1, d)
    args += [gate[:, None, :], lng2, lnb2]
    in_specs += [vec, full(lng2), full(lnb2)]
    out_shape = [jax.ShapeDtypeStruct((t, d), F32)]
    out_specs = [row]
    if nxt is not None:
        args += [nxt[0][:, None, :], nxt[1][:, None, :]]
        in_specs += [vec, vec]
        out_shape.append(jax.ShapeDtypeStruct((t, d), h_dtype))
        out_specs.append(row)
    if router is not None:
        wr, br = router
        args += [wr, br]
        in_specs += [full(wr), full(br)]
        lane_row = pl.BlockSpec((tm, LANES), lambda i: (i, 0))
        out_shape += [jax.ShapeDtypeStruct((t, LANES), jnp.int32), jax.ShapeDtypeStruct((t, LANES), F32)]
        out_specs += [lane_row, lane_row]
    return pl.pallas_call(
        functools.partial(_post_kernel, moe_in=moe_in, has_next=nxt is not None,
                          has_router=router is not None, d=d),
        grid=(t // tm,),
        in_specs=in_specs,
        out_specs=out_specs,
        out_shape=out_shape,
        compiler_params=_cparams(("parallel",)),
        name="deepnorm_post",
    )(*args)


def _gather_kernel(idx_ref, src_ref, out_ref, sem, *, rows):
    base = pl.program_id(0) * rows

    def issue(r, carry):
        tok = idx_ref[0, 0, r]
        pltpu.make_async_copy(src_ref.at[pl.ds(tok, 1), :], out_ref.at[pl.ds(base + r, 1), :], sem).start()
        return carry

    lax.fori_loop(0, rows, issue, 0, unroll=8)
    pltpu.make_async_copy(src_ref.at[pl.ds(0, rows), :], out_ref.at[pl.ds(base, rows), :], sem).wait()


def _gather_rows(src, idx, rows=GATHER_ROWS):
    n = idx.shape[0]
    rows = min(rows, n)
    assert n % rows == 0
    d = src.shape[1]
    return pl.pallas_call(
        functools.partial(_gather_kernel, rows=rows),
        grid=(n // rows,),
        in_specs=[pl.BlockSpec((1, 1, rows), lambda i: (i, 0, 0), memory_space=pltpu.SMEM),
                  pl.BlockSpec(memory_space=pl.ANY)],
        out_specs=pl.BlockSpec(memory_space=pl.ANY),
        out_shape=jax.ShapeDtypeStruct((n, d), src.dtype),
        scratch_shapes=[pltpu.SemaphoreType.DMA],
        compiler_params=_cparams(("arbitrary",)),
        name="row_gather",
    )(idx.reshape(n // rows, 1, rows), src)


def _moe_gu_kernel(te_ref, nv_ref, x_ref, wg_ref, wu_ref, bg_ref, bu_ref, o_ref):
    i = pl.program_id(1)

    @pl.when(i < nv_ref[0])
    def _():
        x = x_ref[...].astype(BF16)
        g = jnp.dot(x, wg_ref[0], preferred_element_type=F32) + bg_ref[0]
        u = jnp.dot(x, wu_ref[0], preferred_element_type=F32) + bu_ref[0]
        g = jnp.minimum(g, SWIGLU_LIMIT)
        u = jnp.clip(u, -SWIGLU_LIMIT, SWIGLU_LIMIT)
        o_ref[...] = (g * jax.nn.sigmoid(SWIGLU_ALPHA * g) * (u + 1.0)).astype(o_ref.dtype)

    @pl.when(i >= nv_ref[0])
    def _():
        o_ref[...] = jnp.zeros_like(o_ref)


def _moe_down_kernel(te_ref, nv_ref, a_ref, wd_ref, bd_ref, o_ref):
    i = pl.program_id(0)

    @pl.when(i < nv_ref[0])
    def _():
        o_ref[...] = jnp.dot(a_ref[...], wd_ref[0], preferred_element_type=F32) + bd_ref[0]

    @pl.when(i >= nv_ref[0])
    def _():
        o_ref[...] = jnp.zeros_like(o_ref)


def _moe_experts(x_sorted, tile_expert, n_valid, w_gu, b_gu, w_d, b_d):
    n_slots, d = x_sorted.shape
    f = w_d.shape[1]
    tm, tn = MOE_TM, MOE_TN
    n_tiles = n_slots // tm
    nj = f // tn
    act = pl.pallas_call(
        _moe_gu_kernel,
        grid_spec=pltpu.PrefetchScalarGridSpec(
            num_scalar_prefetch=2,
            grid=(nj, n_tiles),
            in_specs=[pl.BlockSpec((tm, d), lambda j, i, te, nv: (i, 0)),
                      pl.BlockSpec((1, d, tn), lambda j, i, te, nv: (te[i], 0, j)),
                      pl.BlockSpec((1, d, tn), lambda j, i, te, nv: (te[i], 0, nj + j)),
                      pl.BlockSpec((1, 1, tn), lambda j, i, te, nv: (te[i], 0, j)),
                      pl.BlockSpec((1, 1, tn), lambda j, i, te, nv: (te[i], 0, nj + j))],
            out_specs=pl.BlockSpec((tm, tn), lambda j, i, te, nv: (i, j)),
        ),
        out_shape=jax.ShapeDtypeStruct((n_slots, f), BF16),
        compiler_params=_cparams(("arbitrary", "arbitrary")),
        name="moe_gate_up",
    )(tile_expert, n_valid, x_sorted, w_gu, w_gu, b_gu, b_gu)
    y = pl.pallas_call(
        _moe_down_kernel,
        grid_spec=pltpu.PrefetchScalarGridSpec(
            num_scalar_prefetch=2,
            grid=(n_tiles,),
            in_specs=[pl.BlockSpec((tm, f), lambda i, te, nv: (i, 0)),
                      pl.BlockSpec((1, f, d), lambda i, te, nv: (te[i], 0, 0)),
                      pl.BlockSpec((1, 1, d), lambda i, te, nv: (te[i], 0, 0))],
            out_specs=pl.BlockSpec((tm, d), lambda i, te, nv: (i, 0)),
        ),
        out_shape=jax.ShapeDtypeStruct((n_slots, d), F32),
        compiler_params=_cparams(("arbitrary",)),
        name="moe_down",
    )(tile_expert, n_valid, act, w_d, b_d)
    return y


def _moe_plan(top_idx):
    t = top_idx.shape[0]
    n_items = t * MOE_TOPK
    n_slots = n_items + N_EXPERTS * MOE_TM
    items = top_idx.reshape(n_items)
    onehot = (items[:, None] == jnp.arange(N_EXPERTS, dtype=jnp.int32)[None, :]).astype(jnp.int32)
    csum = jnp.cumsum(onehot, axis=0)
    counts = csum[-1]
    rank = jnp.sum(csum * onehot, axis=1) - 1
    padded = (counts + MOE_TM - 1) // MOE_TM * MOE_TM
    pad_end = jnp.cumsum(padded)
    pad_start = pad_end - padded
    dest = (pad_start[items] + rank).astype(jnp.int32)
    slot_tok = jnp.zeros((n_slots,), jnp.int32).at[dest].set(jnp.arange(n_items, dtype=jnp.int32) // MOE_TOPK)
    tile_start = jnp.arange(n_slots // MOE_TM, dtype=jnp.int32) * MOE_TM
    tile_expert = jnp.minimum(jnp.searchsorted(pad_end, tile_start, side='right'), N_EXPERTS - 1).astype(jnp.int32)
    n_valid = (pad_end[-1] // MOE_TM).astype(jnp.int32).reshape(1)
    return dest, slot_tok, tile_expert, n_valid


def _rope(x, cos_f, sin_s):
    lane = lax.broadcasted_iota(jnp.int32, x.shape, 1)
    half = ROPE_DIM // 2
    partner = jnp.where(lane < half, pltpu.roll(x, LANES - half, 1), pltpu.roll(x, half, 1))
    return x * cos_f + partner * sin_s


def _moba_kernel(q_ref, k_ref, v_ref, cq_ref, sq_ref, ck_ref, sk_ref, o_ref, krot, vbf, kmean, *, n_blk):
    qb = pl.program_id(2)
    blk = MOBA_BLOCK
    scale = HEAD_DIM ** -0.5

    @pl.when(qb == 0)
    def _():
        kmean[...] = jnp.zeros_like(kmean)
        for j in range(n_blk):
            rows = pl.ds(j * blk, blk)
            kr = _rope(k_ref[rows, :], ck_ref[rows, :], sk_ref[rows, :])
            krot[rows, :] = kr.astype(BF16)
            kmean[j:j + 1, :] = jnp.mean(kr, axis=0, keepdims=True)
            vbf[rows, :] = v_ref[rows, :].astype(BF16)

    q = _rope(q_ref[...], cq_ref[...], sq_ref[...])
    qh = q.astype(BF16)
    nt = (((1,), (1,)), ((), ()))

    gate = lax.dot_general(qh, kmean[...].astype(BF16), nt, preferred_element_type=F32)
    lane_i = lax.broadcasted_iota(jnp.int32, gate.shape, 1)
    lane = lane_i.astype(F32)
    past = lane_i < qb
    cur = jnp.where(past, gate, NEG_INF)
    sel = jnp.zeros_like(gate)
    for _ in range(min(MOBA_TOPK, n_blk - 1)):
        m = jnp.max(cur, axis=1, keepdims=True)
        idx = jnp.min(jnp.where(cur == m, lane, float(LANES)), axis=1, keepdims=True)
        hit = lane == idx
        sel = jnp.where(hit & past, 1.0, sel)
        cur = jnp.where(hit, -3.0e38, cur)

    own = pl.ds(pl.multiple_of(qb * blk, blk), blk)
    s = lax.dot_general(qh, krot[own, :], nt, preferred_element_type=F32) * scale
    r_i = lax.broadcasted_iota(jnp.int32, (blk, blk), 0)
    c_i = lax.broadcasted_iota(jnp.int32, (blk, blk), 1)
    s = jnp.where(c_i <= r_i, s, NEG_INF)
    m0 = jnp.max(s, axis=1, keepdims=True)
    p = jnp.exp(s - m0)
    l0 = jnp.sum(p, axis=1, keepdims=True)
    acc0 = jnp.dot(p.astype(BF16), vbf[own, :], preferred_element_type=F32)

    def past_block(j, carry):
        m_run, l_run, acc = carry
        rows = pl.ds(pl.multiple_of(j * blk, blk), blk)
        sj = lax.dot_general(qh, krot[rows, :], nt, preferred_element_type=F32) * scale
        chosen = jnp.max(jnp.where(lane_i == j, sel, 0.0), axis=1, keepdims=True)
        sj = jnp.where(chosen > 0.0, sj, NEG_INF)
        m_new = jnp.maximum(m_run, jnp.max(sj, axis=1, keepdims=True))
        a = jnp.exp(m_run - m_new)
        pj = jnp.exp(sj - m_new)
        l_new = a * l_run + jnp.sum(pj, axis=1, keepdims=True)
        acc_new = a * acc + jnp.dot(pj.astype(BF16), vbf[rows, :], preferred_element_type=F32)
        return m_new, l_new, acc_new

    _, l_fin, acc_fin = lax.fori_loop(0, qb, past_block, (m0, l0, acc0))
    o_ref[...] = (acc_fin / l_fin).astype(o_ref.dtype)


def _moba_attention(qkv, cos_f, sin_s, batch, seq):
    t = qkv.shape[0]
    n_blk = seq // MOBA_BLOCK
    h = MOBA_HEADS
    qspec = pl.BlockSpec((MOBA_BLOCK, HEAD_DIM), lambda b, hh, qb: (b * n_blk + qb, hh))
    kspec = pl.BlockSpec((seq, HEAD_DIM), lambda b, hh, qb: (b, h + hh))
    vspec = pl.BlockSpec((seq, HEAD_DIM), lambda b, hh, qb: (b, 2 * h + hh))
    tq = pl.BlockSpec((MOBA_BLOCK, HEAD_DIM), lambda b, hh, qb: (b * n_blk + qb, 0))
    tk = pl.BlockSpec((seq, HEAD_DIM), lambda b, hh, qb: (b, 0))
    return pl.pallas_call(
        functools.partial(_moba_kernel, n_blk=n_blk),
        grid=(batch, h, n_blk),
        in_specs=[qspec, kspec, vspec, tq, tq, tk, tk],
        out_specs=pl.BlockSpec((MOBA_BLOCK, HEAD_DIM), lambda b, hh, qb: (b * n_blk + qb, hh)),
        out_shape=jax.ShapeDtypeStruct((t, h * HEAD_DIM), BF16),
        scratch_shapes=[pltpu.VMEM((seq, HEAD_DIM), BF16), pltpu.VMEM((seq, HEAD_DIM), BF16),
                        pltpu.VMEM((LANES, HEAD_DIM), F32)],
        compiler_params=_cparams(("parallel", "parallel", "arbitrary")),
        name="moba_attention",
    )(qkv, qkv, qkv, cos_f, sin_s, cos_f, sin_s)


def _rope_tables(positions):
    half = ROPE_DIM // 2
    inv_freq = ROPE_THETA ** (-jnp.arange(0, ROPE_DIM, 2, dtype=F32) / ROPE_DIM)
    ang = positions.astype(F32).reshape(-1, 1) * inv_freq
    cos, sin = jnp.cos(ang), jnp.sin(ang)
    t = ang.shape[0]
    rest = HEAD_DIM - ROPE_DIM
    cos_f = jnp.concatenate([cos, cos, jnp.ones((t, rest), F32)], axis=1)
    sin_s = jnp.concatenate([-sin, sin, jnp.zeros((t, rest), F32)], axis=1)
    return cos_f, sin_s


def _shift_rows(x, k):
    row = lax.broadcasted_iota(jnp.int32, x.shape, 0)
    return jnp.where(row >= k, pltpu.roll(x, k, 0), 0.0)


def _gdn_conv_kernel(x_ref, w_ref, o_ref, *, l2norm):
    x = x_ref[...]
    w = w_ref[...]
    y = x * w[GDN_CONV - 1:GDN_CONV, :]
    for j in range(GDN_CONV - 1):
        y = y + _shift_rows(x, GDN_CONV - 1 - j) * w[j:j + 1, :]
    y = y * jax.nn.sigmoid(y)
    if l2norm:
        for hh in range(y.shape[1] // HEAD_DIM):
            ys = y[:, hh * HEAD_DIM:(hh + 1) * HEAD_DIM]
            ss = jnp.sum(ys * ys, axis=1, keepdims=True)
            o_ref[:, hh * HEAD_DIM:(hh + 1) * HEAD_DIM] = ys * lax.rsqrt(ss + L2_EPS)
    else:
        o_ref[...] = y


def _gdn_conv(proj, conv_w, batch, seq, col0, ncols, l2norm):
    tc = 256
    c0 = col0 // tc
    return pl.pallas_call(
        functools.partial(_gdn_conv_kernel, l2norm=l2norm),
        grid=(batch, ncols // tc),
        in_specs=[pl.BlockSpec((seq, tc), lambda b, c: (b, c0 + c)),
                  pl.BlockSpec((GDN_CONV, tc), lambda b, c: (0, c0 + c))],
        out_specs=pl.BlockSpec((seq, tc), lambda b, c: (b, c)),
        out_shape=jax.ShapeDtypeStruct((batch * seq, ncols), F32),
        compiler_params=_cparams(("parallel", "parallel")),
        name="gdn_conv",
    )(proj, conv_w)


def _gdn_gates_kernel(ba_ref, alog_ref, dtb_ref, gb_ref, gbt_ref):
    x = ba_ref[...]
    lane = lax.broadcasted_iota(jnp.int32, x.shape, 1)
    row = lax.broadcasted_iota(jnp.int32, x.shape, 0)
    xa = x + dtb_ref[...]
    softplus = jnp.maximum(xa, 0.0) + jnp.log(1.0 + jnp.exp(-jnp.abs(xa)))
    g = -jnp.exp(alog_ref[...]) * softplus
    pos = row % GDN_CHUNK
    sh = 1
    while sh < GDN_CHUNK:
        g = g + jnp.where(pos >= sh, pltpu.roll(g, sh, 0), 0.0)
        sh *= 2
    out = jnp.where(lane < GDN_V_HEADS, jax.nn.sigmoid(x), g)
    gb_ref[...] = out
    gbt_ref[...] = out.T


def _gdn_gates(ba, a_log, dt_bias, batch, seq):
    pad = jnp.zeros((LANES - 2 * GDN_V_HEADS,), F32)
    zeros = jnp.zeros((GDN_V_HEADS,), F32)
    alog_row = jnp.concatenate([zeros, a_log.astype(F32), pad]).reshape(1, LANES)
    dtb_row = jnp.concatenate([zeros, dt_bias.astype(F32), pad]).reshape(1, LANES)
    return pl.pallas_call(
        _gdn_gates_kernel,
        grid=(batch,),
        in_specs=[pl.BlockSpec((seq, LANES), lambda b: (b, 0)),
                  pl.BlockSpec((1, LANES), lambda b: (0, 0)),
                  pl.BlockSpec((1, LANES), lambda b: (0, 0))],
        out_specs=[pl.BlockSpec((seq, LANES), lambda b: (b, 0)),
                   pl.BlockSpec((LANES, seq), lambda b: (b, 0))],
        out_shape=[jax.ShapeDtypeStruct((batch * seq, LANES), F32),
                   jax.ShapeDtypeStruct((batch * LANES, seq), F32)],
        compiler_params=_cparams(("parallel",)),
        name="gdn_gates",
    )(ba, alog_row, dtb_row)


def _bmm(a, b):
    return jnp.einsum('nij,njk->nik', a.astype(BF16), b.astype(BF16), preferred_element_type=F32)


def _gdn_core_kernel(q_ref, k_ref, v_ref, z_ref, gb_ref, gbt_ref, nw_ref, o_ref,
                     u_s, w_s, qg_s, kd_s, at_s, eg_s, st_s, *, n_chunks):
    c = GDN_CHUNK
    hk = pl.program_id(1)
    q3 = (q_ref[...] * (HEAD_DIM ** -0.5)).reshape(n_chunks, c, HEAD_DIM)
    k3 = k_ref[...].reshape(n_chunks, c, HEAD_DIM)
    qh, kh = q3.astype(BF16), k3.astype(BF16)
    kk = jnp.einsum('nid,njd->nij', kh, kh, preferred_element_type=F32)
    qk = jnp.einsum('nid,njd->nij', qh, kh, preferred_element_type=F32)
    r_i = lax.broadcasted_iota(jnp.int32, (c, c), 0)
    c_i = lax.broadcasted_iota(jnp.int32, (c, c), 1)
    tril = c_i <= r_i
    strict = c_i < r_i
    eye = (c_i == r_i).astype(F32)
    lane = lax.broadcasted_iota(jnp.int32, gb_ref.shape, 1)
    gb = gb_ref[...]

    for vh in range(2):
        hv = 2 * hk + vh
        beta = jnp.sum(jnp.where(lane == hv, gb, 0.0), axis=1, keepdims=True).reshape(n_chunks, c, 1)
        gcol = jnp.sum(jnp.where(lane == GDN_V_HEADS + hv, gb, 0.0), axis=1, keepdims=True).reshape(n_chunks, c, 1)
        grow = gbt_ref[pl.ds(GDN_V_HEADS + hv, 1), :]
        decay = []
        for n in range(n_chunks):
            diff = gcol[n] - grow[:, n * c:(n + 1) * c]
            decay.append(jnp.exp(jnp.where(tril, diff, NEG_INF)))
        decay = jnp.stack(decay, axis=0)
        m = jnp.where(strict, beta * kk * decay, 0.0)
        at_s[vh] = jnp.where(tril, qk * decay, 0.0)
        t = eye - jnp.where((r_i // 2 == c_i // 2), m, 0.0)
        s = 2
        while s < c:
            off = (r_i // (2 * s) == c_i // (2 * s)) & ((r_i % (2 * s)) >= s) & ((c_i % (2 * s)) < s)
            moff = jnp.where(off, m, 0.0)
            t = t - _bmm(_bmm(t, moff), t)
            s *= 2
        eg = jnp.exp(gcol)
        v3 = v_ref[:, vh * HEAD_DIM:(vh + 1) * HEAD_DIM].reshape(n_chunks, c, HEAD_DIM)
        kb = k3 * beta
        u_s[vh] = _bmm(t, v3 * beta)
        w_s[vh] = _bmm(t, kb * eg)
        qg_s[vh] = q3 * eg
        glast = gcol[:, c - 1:c, :]
        kd_s[vh] = k3 * jnp.exp(glast - gcol)
        eg_s[vh] = jnp.broadcast_to(jnp.exp(glast), (n_chunks, 8, HEAD_DIM))
        st_s[vh] = jnp.zeros((HEAD_DIM, HEAD_DIM), F32)

    nw = nw_ref[...]
    tn = (((0,), (0,)), ((), ()))

    def chunk_step(n, carry):
        rows = pl.ds(pl.multiple_of(n * c, c), c)
        for vh in range(2):
            state = st_s[vh]
            sb = state.astype(BF16)
            v_new = u_s[vh, n] - jnp.dot(w_s[vh, n].astype(BF16), sb, preferred_element_type=F32)
            vb = v_new.astype(BF16)
            o = (jnp.dot(qg_s[vh, n].astype(BF16), sb, preferred_element_type=F32)
                 + jnp.dot(at_s[vh, n].astype(BF16), vb, preferred_element_type=F32))
            st_s[vh] = state * eg_s[vh, n][0:1, :] + lax.dot_general(
                kd_s[vh, n].astype(BF16), vb, tn, preferred_element_type=F32)
            zz = z_ref[rows, vh * HEAD_DIM:(vh + 1) * HEAD_DIM]
            on = o * lax.rsqrt(jnp.mean(o * o, axis=1, keepdims=True) + GDN_NORM_EPS)
            o_ref[rows, vh * HEAD_DIM:(vh + 1) * HEAD_DIM] = (on * nw * (zz * jax.nn.sigmoid(zz))).astype(o_ref.dtype)
        return carry

    lax.fori_loop(0, n_chunks, chunk_step, 0)


def _gdn_core(qk, vc, proj, gb, gbt, norm_w, batch, seq):
    n_chunks = seq // GDN_CHUNK
    hd = HEAD_DIM
    zblk0 = GDN_CONV_DIM // (2 * hd)
    big = lambda: pltpu.VMEM((2, n_chunks, GDN_CHUNK, hd), F32)
    return pl.pallas_call(
        functools.partial(_gdn_core_kernel, n_chunks=n_chunks),
        grid=(batch, GDN_K_HEADS),
        in_specs=[pl.BlockSpec((seq, hd), lambda b, h: (b, h)),
                  pl.BlockSpec((seq, hd), lambda b, h: (b, GDN_K_HEADS + h)),
                  pl.BlockSpec((seq, 2 * hd), lambda b, h: (b, h)),
                  pl.BlockSpec((seq, 2 * hd), lambda b, h: (b, zblk0 + h)),
                  pl.BlockSpec((seq, LANES), lambda b, h: (b, 0)),
                  pl.BlockSpec((LANES, seq), lambda b, h: (b, 0)),
                  pl.BlockSpec((1, hd), lambda b, h: (0, 0))],
        out_specs=pl.BlockSpec((seq, 2 * hd), lambda b, h: (b, h)),
        out_shape=jax.ShapeDtypeStruct((batch * seq, GDN_VALUE_DIM), BF16),
        scratch_shapes=[big(), big(), big(), big(), big(),
                        pltpu.VMEM((2, n_chunks, 8, hd), F32), pltpu.VMEM((2, hd, hd), F32)],
        compiler_params=_cparams(("parallel", "arbitrary")),
        name="gdn_core",
    )(qk, qk, vc, proj, gb, gbt, norm_w.reshape(1, hd).astype(F32))


def _gated_deltanet(h, w_in, conv_w, a_log, dt_bias, norm_w, w_out, batch, seq):
    w_main = w_in[:, :GDN_MAIN_DIM].astype(BF16)
    w_ba = jnp.pad(w_in[:, GDN_MAIN_DIM:], ((0, 0), (0, LANES - 2 * GDN_V_HEADS))).astype(BF16)
    proj = _matmul(h, w_main, out_dtype=F32, tm=1024, tn=1024, name="gdn_in_proj")
    ba = _matmul(h, w_ba, out_dtype=F32, tm=1024, tn=LANES, name="gdn_ba_proj")
    qk = _gdn_conv(proj, conv_w, batch, seq, 0, 2 * GDN_KEY_DIM, True)
    vc = _gdn_conv(proj, conv_w, batch, seq, 2 * GDN_KEY_DIM, GDN_VALUE_DIM, False)
    gb, gbt = _gdn_gates(ba, a_log, dt_bias, batch, seq)
    o = _gdn_core(qk, vc, proj, gb, gbt, norm_w, batch, seq)
    return _matmul(o, w_out.astype(BF16), out_dtype=F32, tm=512, tn=1024, name="gdn_out_proj")


def _adaln(c, w, b):
    m = _matmul(c, w.astype(BF16), out_dtype=F32, tm=c.shape[0], tn=1024, bias=b, silu_a=True, name="adaln")
    d = w.shape[0]
    return m[:, :d], m[:, d:2 * d], m[:, 2 * d:]


def _moe_sublayer(x, h, top_idx, gate_w, gate_c, lng, lnb, seq, w_gu, b_gu, w_d, b_d, nxt):
    t, d = x.shape
    dest, slot_tok, tile_expert, n_valid = _moe_plan(top_idx[:, :MOE_TOPK])
    x_sorted = _gather_rows(h, slot_tok)
    y_sorted = _moe_experts(x_sorted, tile_expert, n_valid, w_gu.astype(BF16),
                            b_gu[:, None, :], w_d.astype(BF16), b_d[:, None, :])
    y4 = _gather_rows(y_sorted, dest).reshape(t, MOE_TOPK * d)
    return _post(x, y4, gate_c, lng, lnb, seq, w4=gate_w, nxt=nxt)


def kernel(x, c, positions, ada_w, ada_b, ln_g, ln_b, moba_w_qkv, moba_w_o, gdn_w_in, gdn_conv_w, gdn_a_log,
           gdn_dt_bias, gdn_norm_w, gdn_w_out, router_w, router_b, moe_w_gate_up, moe_b_gate_up, moe_w_down,
           moe_b_down):
    batch, seq, d = x.shape
    t = batch * seq
    xs = x.reshape(t, d)
    cos_f, sin_s = _rope_tables(positions)
    mods = [[_adaln(c, ada_w[i, j], ada_b[i, j]) for j in range(2)] for i in range(DEPTH)]

    def router(i):
        wr = jnp.pad(router_w[i], ((0, 0), (0, LANES - N_EXPERTS)))
        br = jnp.concatenate([router_b[i], jnp.full((LANES - N_EXPERTS,), NEG_INF, F32)]).reshape(1, LANES)
        return wr, br

    for i in range(DEPTH):
        shift, scale, gate = mods[i][0]
        if i == 0:
            h = _modulate(xs, scale, shift, seq, BF16)
        if i % 2 == 0:
            qkv = _matmul(h, moba_w_qkv[i // 2].astype(BF16), out_dtype=F32, tm=1024, tn=1024, name="moba_qkv")
            o = _moba_attention(qkv, cos_f, sin_s, batch, seq)
            y = _matmul(o, moba_w_o[i // 2].astype(BF16), out_dtype=F32, tm=1024, tn=1024, name="moba_out")
        else:
            j = i // 2
            y = _gated_deltanet(h, gdn_w_in[j], gdn_conv_w[j], gdn_a_log[j], gdn_dt_bias[j], gdn_norm_w[j],
                                gdn_w_out[j], batch, seq)
        shift2, scale2, gate2 = mods[i][1]
        xs, h32, top_idx, gate_w = _post(xs, y, gate, ln_g[i, 0], ln_b[i, 0], seq, nxt=(scale2, shift2),
                                         router=router(i), h_dtype=F32)
        if i + 1 < DEPTH:
            shift_n, scale_n, _ = mods[i + 1][0]
            xs, h = _moe_sublayer(xs, h32, top_idx, gate_w, gate2, ln_g[i, 1], ln_b[i, 1], seq, moe_w_gate_up[i],
                                  moe_b_gate_up[i], moe_w_down[i], moe_b_down[i], (scale_n, shift_n))
        else:
            (xs,) = _post_last(xs, h32, top_idx, gate_w, gate2, ln_g[i, 1], ln_b[i, 1], seq, moe_w_gate_up[i],
                               moe_b_gate_up[i], moe_w_down[i], moe_b_down[i])
    return xs.reshape(batch, seq, d)


def _post_last(x, h, top_idx, gate_w, gate_c, lng, lnb, seq, w_gu, b_gu, w_d, b_d):
    return _moe_sublayer(x, h, top_idx, gate_w, gate_c, lng, lnb, seq, w_gu, b_gu, w_d, b_d, None)
```

```python
import functools
import math

import jax
import jax.numpy as jnp
from jax import lax
from jax.experimental import pallas as pl
from jax.experimental.pallas import tpu as pltpu

F32 = jnp.float32
BF16 = jnp.bfloat16

D_MODEL = 2048
DEPTH = 2
DEEPNORM_ALPHA = float((2 * DEPTH) ** 0.25)
LN_EPS = 1e-5
MOBA_HEADS = 16
HEAD_DIM = 128
MOBA_BLOCK = 256
MOBA_TOPK = 3
ROPE_THETA = 500000.0
ROPE_DIM = HEAD_DIM // 4
NEG_INF = -1e30
GDN_K_HEADS = 16
GDN_V_HEADS = 32
GDN_KEY_DIM = GDN_K_HEADS * HEAD_DIM
GDN_VALUE_DIM = GDN_V_HEADS * HEAD_DIM
GDN_CONV_DIM = 2 * GDN_KEY_DIM + GDN_VALUE_DIM
GDN_MAIN_DIM = GDN_CONV_DIM + GDN_VALUE_DIM
GDN_CONV = 4
GDN_CHUNK = 128
GDN_NORM_EPS = 1e-6
L2_EPS = 1e-6
N_EXPERTS = 32
MOE_TOPK = 4
SWIGLU_ALPHA = 1.702
SWIGLU_LIMIT = 7.0

LANES = 128
VMEM_LIMIT = 50 * 1024 * 1024

MOE_TM = 512
MOE_TN = 1024
GATHER_ROWS = 1024
POST_TM = 256


def _cparams(sem):
    return pltpu.CompilerParams(dimension_semantics=sem, vmem_limit_bytes=VMEM_LIMIT)


def _mm_kernel(*refs, silu_a, has_bias):
    if has_bias:
        a_ref, b_ref, bias_ref, o_ref = refs
    else:
        a_ref, b_ref, o_ref = refs
    a = a_ref[...]
    if silu_a:
        a = a.astype(F32)
        a = a * jax.nn.sigmoid(a)
    acc = jnp.dot(a.astype(BF16), b_ref[...], preferred_element_type=F32)
    if has_bias:
        acc = acc + bias_ref[...]
    o_ref[...] = acc.astype(o_ref.dtype)


def _matmul(a, b, *, out_dtype, tm, tn, bias=None, silu_a=False, name="mm"):
    m, k = a.shape
    n = b.shape[1]
    tm = min(tm, m)
    tn = min(tn, n)
    assert m % tm == 0 and n % tn == 0, (m, n, tm, tn)
    in_specs = [pl.BlockSpec((tm, k), lambda i, j: (i, 0)),
                pl.BlockSpec((k, tn), lambda i, j: (0, j))]
    args = [a, b]
    if bias is not None:
        in_specs.append(pl.BlockSpec((1, tn), lambda i, j: (0, j)))
        args.append(bias.reshape(1, n).astype(F32))
    return pl.pallas_call(
        functools.partial(_mm_kernel, silu_a=silu_a, has_bias=bias is not None),
        grid=(m // tm, n // tn),
        in_specs=in_specs,
        out_specs=pl.BlockSpec((tm, tn), lambda i, j: (i, j)),
        out_shape=jax.ShapeDtypeStruct((m, n), out_dtype),
        compiler_params=_cparams(("parallel", "parallel")),
        name=name,
    )(*args)


def _modulate_kernel(x_ref, scale_ref, shift_ref, h_ref):
    h = x_ref[...] * (1.0 + scale_ref[0]) + shift_ref[0]
    h_ref[...] = h.astype(h_ref.dtype)


def _modulate(x, scale, shift, seq, out_dtype):
    t, d = x.shape
    tm = POST_TM
    per_b = seq // tm
    vec = pl.BlockSpec((1, 1, d), lambda i: (i // per_b, 0, 0))
    return pl.pallas_call(
        _modulate_kernel,
        grid=(t // tm,),
        in_specs=[pl.BlockSpec((tm, d), lambda i: (i, 0)), vec, vec],
        out_specs=pl.BlockSpec((tm, d), lambda i: (i, 0)),
        out_shape=jax.ShapeDtypeStruct((t, d), out_dtype),
        compiler_params=_cparams(("parallel",)),
        name="modulate",
    )(x, scale[:, None, :], shift[:, None, :])


def _top4(logits):
    lane = lax.broadcasted_iota(jnp.int32, logits.shape, 1).astype(F32)
    cur = logits
    vals, idxs = [], []
    for _ in range(MOE_TOPK):
        m = jnp.max(cur, axis=1, keepdims=True)
        idx = jnp.min(jnp.where(cur == m, lane, float(LANES)), axis=1, keepdims=True)
        vals.append(m)
        idxs.append(idx)
        cur = jnp.where(lane == idx, -3.0e38, cur)
    exps = [jnp.exp(v - vals[0]) for v in vals]
    denom = exps[0] + exps[1] + exps[2] + exps[3]
    idx_out = jnp.zeros_like(logits)
    gate_out = jnp.zeros_like(logits)
    for k in range(MOE_TOPK):
        idx_out = jnp.where(lane == float(k), idxs[k], idx_out)
        gate_out = jnp.where(lane == float(k), exps[k] / denom, gate_out)
    return idx_out.astype(jnp.int32), gate_out


def _store_tiled(ref, val, group=1, member=0):
    n, d = val.shape
    s_per = d // LANES
    for s in range(s_per):
        ref[pl.ds(member * s_per + s, n, stride=group * s_per), :] = val[:, s * LANES:(s + 1) * LANES]


def _load_tiled(ref, member, group, d=D_MODEL):
    s_per = d // LANES
    n = ref.shape[0] // (group * s_per)
    return jnp.concatenate([ref[pl.ds(member * s_per + s, n, stride=group * s_per), :] for s in range(s_per)],
                           axis=1)


def _post_kernel(*refs, moe_in, has_next, has_router, h_tiled, d):
    it = iter(refs)
    x_ref = next(it)
    y_ref = next(it)
    w4_ref = next(it) if moe_in else None
    gate_ref = next(it)
    lng_ref = next(it)
    lnb_ref = next(it)
    if has_next:
        scale_ref = next(it)
        shift_ref = next(it)
    if has_router:
        wr_ref = next(it)
        br_ref = next(it)
    xo_ref = next(it)
    if has_next:
        h_ref = next(it)
    if has_router:
        idx_ref = next(it)
        gw_ref = next(it)

    if moe_in:
        w4 = w4_ref[...]
        y = w4[:, 0:1] * _load_tiled(y_ref, 0, MOE_TOPK)
        for k in range(1, MOE_TOPK):
            y = y + w4[:, k:k + 1] * _load_tiled(y_ref, k, MOE_TOPK)
    else:
        y = y_ref[...].astype(F32)
    z = DEEPNORM_ALPHA * x_ref[...] + (1.0 + gate_ref[0]) * y
    mu = jnp.mean(z, axis=1, keepdims=True)
    zc = z - mu
    var = jnp.mean(zc * zc, axis=1, keepdims=True)
    xn = zc * lax.rsqrt(var + LN_EPS) * lng_ref[...] + lnb_ref[...]
    xo_ref[...] = xn
    if has_next:
        h = xn * (1.0 + scale_ref[0]) + shift_ref[0]
        if h_tiled:
            _store_tiled(h_ref, h)
        else:
            h_ref[...] = h.astype(h_ref.dtype)
    if has_router:
        logits = jnp.dot(h, wr_ref[...], precision=lax.Precision.HIGHEST,
                         preferred_element_type=F32) + br_ref[...]
        idx, gw = _top4(logits)
        idx_ref[...] = idx
        gw_ref[...] = gw


def _post(x, y, gate, lng, lnb, seq, *, w4=None, nxt=None, router=None, h_tiled=False):
    t, d = x.shape
    tm = POST_TM
    per_b = seq // tm
    s_per = d // LANES
    row = pl.BlockSpec((tm, d), lambda i: (i, 0))
    vec = pl.BlockSpec((1, 1, d), lambda i: (i // per_b, 0, 0))
    full = lambda a: pl.BlockSpec(a.shape, lambda i: (0,) * a.ndim)
    moe_in = w4 is not None
    args = [x, y]
    if moe_in:
        in_specs = [row, pl.BlockSpec((tm * MOE_TOPK * s_per, LANES), lambda i: (i, 0))]
    else:
        in_specs = [row, row]
    if moe_in:
        args.append(w4)
        in_specs.append(pl.BlockSpec((tm, LANES), lambda i: (i, 0)))
    lng2, lnb2 = lng.reshape(1, d), lnb.reshape(1, d)
    args += [gate[:, None, :], lng2, lnb2]
    in_specs += [vec, full(lng2), full(lnb2)]
    out_shape = [jax.ShapeDtypeStruct((t, d), F32)]
    out_specs = [row]
    if nxt is not None:
        args += [nxt[0][:, None, :], nxt[1][:, None, :]]
        in_specs += [vec, vec]
        if h_tiled:
            out_shape.append(jax.ShapeDtypeStruct((t * s_per, LANES), F32))
            out_specs.append(pl.BlockSpec((tm * s_per, LANES), lambda i: (i, 0)))
        else:
            out_shape.append(jax.ShapeDtypeStruct((t, d), BF16))
            out_specs.append(row)
    if router is not None:
        wr, br = router
        args += [wr, br]
        in_specs += [full(wr), full(br)]
        lane_row = pl.BlockSpec((tm, LANES), lambda i: (i, 0))
        out_shape += [jax.ShapeDtypeStruct((t, LANES), jnp.int32), jax.ShapeDtypeStruct((t, LANES), F32)]
        out_specs += [lane_row, lane_row]
    return pl.pallas_call(
        functools.partial(_post_kernel, moe_in=moe_in, has_next=nxt is not None,
                          has_router=router is not None, h_tiled=h_tiled, d=d),
        grid=(t // tm,),
        in_specs=in_specs,
        out_specs=out_specs,
        out_shape=out_shape,
        compiler_params=_cparams(("parallel",)),
        name="deepnorm_post",
    )(*args)


def _gather_kernel(idx_ref, src_ref, out_ref, sems, *, rows, s_per):
    i = pl.program_id(0)
    last = pl.num_programs(0) - 1
    base = i * rows

    def issue(r, carry):
        tok = idx_ref[0, 0, r]
        pltpu.make_async_copy(src_ref.at[pl.ds(tok * s_per, s_per), :],
                              out_ref.at[pl.ds((base + r) * s_per, s_per), :], sems.at[i % 2]).start()
        return carry

    lax.fori_loop(0, rows, issue, 0, unroll=8)

    def wait_step(step):
        pltpu.make_async_copy(src_ref.at[pl.ds(0, rows * s_per), :],
                              out_ref.at[pl.ds(step * rows * s_per, rows * s_per), :], sems.at[step % 2]).wait()

    @pl.when(i > 0)
    def _():
        wait_step(i - 1)

    @pl.when(i == last)
    def _():
        wait_step(i)


def _gather_rows(src, idx, d=D_MODEL, rows=GATHER_ROWS):
    n = idx.shape[0]
    rows = min(rows, n)
    assert n % rows == 0
    s_per = d // LANES
    return pl.pallas_call(
        functools.partial(_gather_kernel, rows=rows, s_per=s_per),
        grid=(n // rows,),
        in_specs=[pl.BlockSpec((1, 1, rows), lambda i: (i, 0, 0), memory_space=pltpu.SMEM),
                  pl.BlockSpec(memory_space=pl.ANY)],
        out_specs=pl.BlockSpec(memory_space=pl.ANY),
        out_shape=jax.ShapeDtypeStruct((n * s_per, LANES), src.dtype),
        scratch_shapes=[pltpu.SemaphoreType.DMA((2,))],
        compiler_params=_cparams(("arbitrary",)),
        name="row_gather",
    )(idx.reshape(n // rows, 1, rows), src)


def _moe_gu_kernel(te_ref, nv_ref, x_ref, wg_ref, wu_ref, bg_ref, bu_ref, o_ref):
    i = pl.program_id(1)

    @pl.when(i < nv_ref[0])
    def _():
        x = _load_tiled(x_ref, 0, 1).astype(BF16)
        g = jnp.dot(x, wg_ref[0], preferred_element_type=F32) + bg_ref[0]
        u = jnp.dot(x, wu_ref[0], preferred_element_type=F32) + bu_ref[0]
        g = jnp.minimum(g, SWIGLU_LIMIT)
        u = jnp.clip(u, -SWIGLU_LIMIT, SWIGLU_LIMIT)
        o_ref[...] = (g * jax.nn.sigmoid(SWIGLU_ALPHA * g) * (u + 1.0)).astype(o_ref.dtype)

    @pl.when(i >= nv_ref[0])
    def _():
        o_ref[...] = jnp.zeros_like(o_ref)


def _moe_down_kernel(te_ref, nv_ref, a_ref, wd_ref, bd_ref, o_ref):
    i = pl.program_id(0)

    @pl.when(i < nv_ref[0])
    def _():
        _store_tiled(o_ref, jnp.dot(a_ref[...], wd_ref[0], preferred_element_type=F32) + bd_ref[0])

    @pl.when(i >= nv_ref[0])
    def _():
        o_ref[...] = jnp.zeros_like(o_ref)


def _moe_experts(x_sorted, tile_expert, n_valid, w_gu, b_gu, w_d, b_d):
    f, d = w_d.shape[1], w_d.shape[2]
    s_per = d // LANES
    n_slots = x_sorted.shape[0] // s_per
    tm, tn = MOE_TM, MOE_TN
    n_tiles = n_slots // tm
    nj = f // tn
    act = pl.pallas_call(
        _moe_gu_kernel,
        grid_spec=pltpu.PrefetchScalarGridSpec(
            num_scalar_prefetch=2,
            grid=(nj, n_tiles),
            in_specs=[pl.BlockSpec((tm * s_per, LANES), lambda j, i, te, nv: (i, 0)),
                      pl.BlockSpec((1, d, tn), lambda j, i, te, nv: (te[i], 0, j)),
                      pl.BlockSpec((1, d, tn), lambda j, i, te, nv: (te[i], 0, nj + j)),
                      pl.BlockSpec((1, 1, tn), lambda j, i, te, nv: (te[i], 0, j)),
                      pl.BlockSpec((1, 1, tn), lambda j, i, te, nv: (te[i], 0, nj + j))],
            out_specs=pl.BlockSpec((tm, tn), lambda j, i, te, nv: (i, j)),
        ),
        out_shape=jax.ShapeDtypeStruct((n_slots, f), BF16),
        compiler_params=_cparams(("arbitrary", "arbitrary")),
        name="moe_gate_up",
    )(tile_expert, n_valid, x_sorted, w_gu, w_gu, b_gu, b_gu)
    y = pl.pallas_call(
        _moe_down_kernel,
        grid_spec=pltpu.PrefetchScalarGridSpec(
            num_scalar_prefetch=2,
            grid=(n_tiles,),
            in_specs=[pl.BlockSpec((tm, f), lambda i, te, nv: (i, 0)),
                      pl.BlockSpec((1, f, d), lambda i, te, nv: (te[i], 0, 0)),
                      pl.BlockSpec((1, 1, d), lambda i, te, nv: (te[i], 0, 0))],
            out_specs=pl.BlockSpec((tm * s_per, LANES), lambda i, te, nv: (i, 0)),
        ),
        out_shape=jax.ShapeDtypeStruct((n_slots * s_per, LANES), F32),
        compiler_params=_cparams(("arbitrary",)),
        name="moe_down",
    )(tile_expert, n_valid, act, w_d, b_d)
    return y


def _moe_plan(top_idx):
    t = top_idx.shape[0]
    n_items = t * MOE_TOPK
    n_slots = n_items + N_EXPERTS * MOE_TM
    items = top_idx.reshape(n_items)
    onehot = (items[:, None] == jnp.arange(N_EXPERTS, dtype=jnp.int32)[None, :]).astype(jnp.int32)
    csum = jnp.cumsum(onehot, axis=0)
    counts = csum[-1]
    rank = jnp.sum(csum * onehot, axis=1) - 1
    padded = (counts + MOE_TM - 1) // MOE_TM * MOE_TM
    pad_end = jnp.cumsum(padded)
    pad_start = pad_end - padded
    dest = (pad_start[items] + rank).astype(jnp.int32)
    slot_tok = jnp.zeros((n_slots,), jnp.int32).at[dest].set(jnp.arange(n_items, dtype=jnp.int32) // MOE_TOPK)
    tile_start = jnp.arange(n_slots // MOE_TM, dtype=jnp.int32) * MOE_TM
    tile_expert = jnp.minimum(jnp.searchsorted(pad_end, tile_start, side='right'), N_EXPERTS - 1).astype(jnp.int32)
    n_valid = (pad_end[-1] // MOE_TM).astype(jnp.int32).reshape(1)
    return dest, slot_tok, tile_expert, n_valid


def _rope(x, cos_f, sin_s):
    lane = lax.broadcasted_iota(jnp.int32, x.shape, 1)
    half = ROPE_DIM // 2
    partner = jnp.where(lane < half, pltpu.roll(x, LANES - half, 1), pltpu.roll(x, half, 1))
    return x * cos_f + partner * sin_s


def _moba_kernel(q_ref, k_ref, v_ref, cos_ref, sin_ref, o_ref, *, n_blk):
    blk = MOBA_BLOCK
    cos_f, sin_s = cos_ref[...], sin_ref[...]
    kr = _rope(k_ref[...], cos_f, sin_s)
    means = [jnp.mean(kr[j * blk:(j + 1) * blk], axis=0, keepdims=True) for j in range(n_blk)]
    n_rows = max(n_blk, 8)
    if n_rows > n_blk:
        means.append(jnp.zeros((n_rows - n_blk, HEAD_DIM), F32))
    kmean = jnp.concatenate(means, axis=0).astype(BF16)
    ks = (kr * (HEAD_DIM ** -0.5)).astype(BF16)
    vt = v_ref[...].T.astype(BF16)
    qt = _rope(q_ref[...], cos_f, sin_s).T.astype(BF16)

    gate = jnp.dot(kmean, qt, preferred_element_type=F32)
    row_i = lax.broadcasted_iota(jnp.int32, gate.shape, 0)
    rowf = row_i.astype(F32)
    past = row_i < lax.broadcasted_iota(jnp.int32, gate.shape, 1) // blk
    cur = jnp.where(past, gate, NEG_INF)
    sel = jnp.zeros_like(gate)
    for _ in range(min(MOBA_TOPK, n_blk - 1)):
        m = jnp.max(cur, axis=0, keepdims=True)
        idx = jnp.min(jnp.where(cur == m, rowf, float(LANES)), axis=0, keepdims=True)
        hit = rowf == idx
        sel = jnp.where(hit & past, 1.0, sel)
        cur = jnp.where(hit, -3.0e38, cur)

    r_i = lax.broadcasted_iota(jnp.int32, (blk, blk), 0)
    c_i = lax.broadcasted_iota(jnp.int32, (blk, blk), 1)
    outs = []
    for qb in range(n_blk):
        cols = slice(qb * blk, (qb + 1) * blk)
        nk = (qb + 1) * blk
        s = jnp.dot(ks[:nk], qt[:, cols], preferred_element_type=F32)
        parts = [jnp.where(sel[j:j + 1, cols] > 0.0, s[j * blk:(j + 1) * blk], NEG_INF) for j in range(qb)]
        parts.append(jnp.where(r_i <= c_i, s[qb * blk:], NEG_INF))
        s = jnp.concatenate(parts, axis=0)
        p = jnp.exp(s - jnp.max(s, axis=0, keepdims=True))
        l = jnp.sum(p, axis=0, keepdims=True)
        acc = jnp.dot(vt[:, :nk], p.astype(BF16), preferred_element_type=F32)
        outs.append(acc / l)
    o_ref[...] = jnp.concatenate(outs, axis=1).T.astype(o_ref.dtype)


def _moba_attention(qkv, cos_f, sin_s, batch, seq):
    t = qkv.shape[0]
    n_blk = seq // MOBA_BLOCK
    h = MOBA_HEADS
    head = lambda off: pl.BlockSpec((seq, HEAD_DIM), lambda b, hh: (b, off + hh))
    table = pl.BlockSpec((seq, HEAD_DIM), lambda b, hh: (b, 0))
    return pl.pallas_call(
        functools.partial(_moba_kernel, n_blk=n_blk),
        grid=(batch, h),
        in_specs=[head(0), head(h), head(2 * h), table, table],
        out_specs=head(0),
        out_shape=jax.ShapeDtypeStruct((t, h * HEAD_DIM), BF16),
        compiler_params=_cparams(("parallel", "parallel")),
        name="moba_attention",
    )(qkv, qkv, qkv, cos_f, sin_s)


def _rope_tables(positions):
    half = ROPE_DIM // 2
    inv_freq = ROPE_THETA ** (-jnp.arange(0, ROPE_DIM, 2, dtype=F32) / ROPE_DIM)
    ang = positions.astype(F32).reshape(-1, 1) * inv_freq
    cos, sin = jnp.cos(ang), jnp.sin(ang)
    t = ang.shape[0]
    rest = HEAD_DIM - ROPE_DIM
    cos_f = jnp.concatenate([cos, cos, jnp.ones((t, rest), F32)], axis=1)
    sin_s = jnp.concatenate([-sin, sin, jnp.zeros((t, rest), F32)], axis=1)
    return cos_f, sin_s


def _shift_rows(x, k):
    row = lax.broadcasted_iota(jnp.int32, x.shape, 0)
    return jnp.where(row >= k, pltpu.roll(x, k, 0), 0.0)


def _gdn_conv_kernel(x_ref, w_ref, o_ref, *, l2norm):
    x = x_ref[...]
    w = w_ref[...]
    y = x * w[GDN_CONV - 1:GDN_CONV, :]
    for j in range(GDN_CONV - 1):
        y = y + _shift_rows(x, GDN_CONV - 1 - j) * w[j:j + 1, :]
    y = y * jax.nn.sigmoid(y)
    if l2norm:
        for hh in range(y.shape[1] // HEAD_DIM):
            ys = y[:, hh * HEAD_DIM:(hh + 1) * HEAD_DIM]
            ss = jnp.sum(ys * ys, axis=1, keepdims=True)
            o_ref[:, hh * HEAD_DIM:(hh + 1) * HEAD_DIM] = ys * lax.rsqrt(ss + L2_EPS)
    else:
        o_ref[...] = y


def _gdn_conv(proj, conv_w, batch, seq, col0, ncols, l2norm):
    tc = 256
    c0 = col0 // tc
    return pl.pallas_call(
        functools.partial(_gdn_conv_kernel, l2norm=l2norm),
        grid=(batch, ncols // tc),
        in_specs=[pl.BlockSpec((seq, tc), lambda b, c: (b, c0 + c)),
                  pl.BlockSpec((GDN_CONV, tc), lambda b, c: (0, c0 + c))],
        out_specs=pl.BlockSpec((seq, tc), lambda b, c: (b, c)),
        out_shape=jax.ShapeDtypeStruct((batch * seq, ncols), F32),
        compiler_params=_cparams(("parallel", "parallel")),
        name="gdn_conv",
    )(proj, conv_w)


def _gdn_gates_kernel(ba_ref, alog_ref, dtb_ref, gb_ref, gbt_ref):
    x = ba_ref[...]
    lane = lax.broadcasted_iota(jnp.int32, x.shape, 1)
    row = lax.broadcasted_iota(jnp.int32, x.shape, 0)
    xa = x + dtb_ref[...]
    softplus = jnp.maximum(xa, 0.0) + jnp.log(1.0 + jnp.exp(-jnp.abs(xa)))
    g = -jnp.exp(alog_ref[...]) * softplus
    pos = row % GDN_CHUNK
    sh = 1
    while sh < GDN_CHUNK:
        g = g + jnp.where(pos >= sh, pltpu.roll(g, sh, 0), 0.0)
        sh *= 2
    out = jnp.where(lane < GDN_V_HEADS, jax.nn.sigmoid(x), g)
    gb_ref[...] = out
    gbt_ref[...] = out.T


def _gdn_gates(ba, a_log, dt_bias, batch, seq):
    pad = jnp.zeros((LANES - 2 * GDN_V_HEADS,), F32)
    zeros = jnp.zeros((GDN_V_HEADS,), F32)
    alog_row = jnp.concatenate([zeros, a_log.astype(F32), pad]).reshape(1, LANES)
    dtb_row = jnp.concatenate([zeros, dt_bias.astype(F32), pad]).reshape(1, LANES)
    return pl.pallas_call(
        _gdn_gates_kernel,
        grid=(batch,),
        in_specs=[pl.BlockSpec((seq, LANES), lambda b: (b, 0)),
                  pl.BlockSpec((1, LANES), lambda b: (0, 0)),
                  pl.BlockSpec((1, LANES), lambda b: (0, 0))],
        out_specs=[pl.BlockSpec((seq, LANES), lambda b: (b, 0)),
                   pl.BlockSpec((LANES, seq), lambda b: (b, 0))],
        out_shape=[jax.ShapeDtypeStruct((batch * seq, LANES), F32),
                   jax.ShapeDtypeStruct((batch * LANES, seq), F32)],
        compiler_params=_cparams(("parallel",)),
        name="gdn_gates",
    )(ba, alog_row, dtb_row)


def _bmm(a, b):
    return jnp.einsum('nij,njk->nik', a.astype(BF16), b.astype(BF16), preferred_element_type=F32)


def _gdn_core_kernel(q_ref, k_ref, v_ref, z_ref, gb_ref, gbt_ref, nw_ref, o_ref,
                     b_s, c_s, qe_s, o0_s, eg_s, st_s, *, n_chunks):
    c = GDN_CHUNK
    hk = pl.program_id(1)
    q3 = (q_ref[...] * (HEAD_DIM ** -0.5)).reshape(n_chunks, c, HEAD_DIM)
    k2 = k_ref[...]
    k3 = k2.reshape(n_chunks, c, HEAD_DIM)
    kt = k2.T
    qh, kh = q3.astype(BF16), k3.astype(BF16)
    kk = jnp.einsum('nid,njd->nij', kh, kh, preferred_element_type=F32)
    qk = jnp.einsum('nid,njd->nij', qh, kh, preferred_element_type=F32)
    r_i = lax.broadcasted_iota(jnp.int32, (c, c), 0)
    c_i = lax.broadcasted_iota(jnp.int32, (c, c), 1)
    tril = c_i <= r_i
    strict = c_i < r_i
    eye = (c_i == r_i).astype(F32)
    lane = lax.broadcasted_iota(jnp.int32, gb_ref.shape, 1)
    gb = gb_ref[...]

    for vh in range(2):
        hv = 2 * hk + vh
        beta = jnp.sum(jnp.where(lane == hv, gb, 0.0), axis=1, keepdims=True).reshape(n_chunks, c, 1)
        gcol = jnp.sum(jnp.where(lane == GDN_V_HEADS + hv, gb, 0.0), axis=1, keepdims=True).reshape(n_chunks, c, 1)
        grow = gbt_ref[pl.ds(GDN_V_HEADS + hv, 1), :]
        decay, kdt = [], []
        for n in range(n_chunks):
            gr = grow[:, n * c:(n + 1) * c]
            diff = gcol[n] - gr
            decay.append(jnp.exp(jnp.where(tril, diff, NEG_INF)))
            kdt.append(kt[:, n * c:(n + 1) * c] * jnp.exp(gr[:, c - 1:c] - gr))
        decay = jnp.stack(decay, axis=0)
        kdt = jnp.stack(kdt, axis=0)
        m = jnp.where(strict, beta * kk * decay, 0.0)
        attn = jnp.where(tril, qk * decay, 0.0)
        t = eye - jnp.where((r_i // 2 == c_i // 2), m, 0.0)
        s = 2
        while s < c:
            off = (r_i // (2 * s) == c_i // (2 * s)) & ((r_i % (2 * s)) >= s) & ((c_i % (2 * s)) < s)
            moff = jnp.where(off, m, 0.0)
            t = t - _bmm(_bmm(t, moff), t)
            s *= 2
        eg = jnp.exp(gcol)
        v3 = v_ref[:, vh * HEAD_DIM:(vh + 1) * HEAD_DIM].reshape(n_chunks, c, HEAD_DIM)
        uw = _bmm(t, jnp.concatenate([v3 * beta, k3 * (beta * eg)], axis=-1))
        bc = _bmm(kdt, uw)
        ao = _bmm(attn, uw)
        b_s[vh] = bc[:, :, :HEAD_DIM]
        c_s[vh] = bc[:, :, HEAD_DIM:].astype(BF16)
        o0_s[vh] = ao[:, :, :HEAD_DIM]
        qe_s[vh] = (q3 * eg - ao[:, :, HEAD_DIM:]).astype(BF16)
        glast = gcol[:, c - 1:c, :]
        eg_s[vh] = jnp.broadcast_to(jnp.exp(glast), (n_chunks, 8, HEAD_DIM))
        st_s[vh] = jnp.zeros((HEAD_DIM, HEAD_DIM), F32)

    nw = nw_ref[...]

    def chunk_step(n, carry):
        rows = pl.ds(pl.multiple_of(n * c, c), c)
        for vh in range(2):
            state = st_s[vh]
            sb = state.astype(BF16)
            o = jnp.dot(qe_s[vh, n], sb, preferred_element_type=F32) + o0_s[vh, n]
            st_s[vh] = (state * eg_s[vh, n][0:1, :] + b_s[vh, n]
                        - jnp.dot(c_s[vh, n], sb, preferred_element_type=F32))
            zz = z_ref[rows, vh * HEAD_DIM:(vh + 1) * HEAD_DIM]
            on = o * lax.rsqrt(jnp.mean(o * o, axis=1, keepdims=True) + GDN_NORM_EPS)
            o_ref[rows, vh * HEAD_DIM:(vh + 1) * HEAD_DIM] = (on * nw * (zz * jax.nn.sigmoid(zz))).astype(o_ref.dtype)
        return carry

    lax.fori_loop(0, n_chunks, chunk_step, 0)


def _gdn_core(qk, vc, proj, gb, gbt, norm_w, batch, seq):
    n_chunks = seq // GDN_CHUNK
    hd = HEAD_DIM
    zblk0 = GDN_CONV_DIM // (2 * hd)
    big = lambda dt: pltpu.VMEM((2, n_chunks, GDN_CHUNK, hd), dt)
    return pl.pallas_call(
        functools.partial(_gdn_core_kernel, n_chunks=n_chunks),
        grid=(batch, GDN_K_HEADS),
        in_specs=[pl.BlockSpec((seq, hd), lambda b, h: (b, h)),
                  pl.BlockSpec((seq, hd), lambda b, h: (b, GDN_K_HEADS + h)),
                  pl.BlockSpec((seq, 2 * hd), lambda b, h: (b, h)),
                  pl.BlockSpec((seq, 2 * hd), lambda b, h: (b, zblk0 + h)),
                  pl.BlockSpec((seq, LANES), lambda b, h: (b, 0)),
                  pl.BlockSpec((LANES, seq), lambda b, h: (b, 0)),
                  pl.BlockSpec((1, hd), lambda b, h: (0, 0))],
        out_specs=pl.BlockSpec((seq, 2 * hd), lambda b, h: (b, h)),
        out_shape=jax.ShapeDtypeStruct((batch * seq, GDN_VALUE_DIM), BF16),
        scratch_shapes=[big(F32), big(BF16), big(BF16), big(F32),
                        pltpu.VMEM((2, n_chunks, 8, hd), F32), pltpu.VMEM((2, hd, hd), F32)],
        compiler_params=_cparams(("parallel", "arbitrary")),
        name="gdn_core",
    )(qk, qk, vc, proj, gb, gbt, norm_w.reshape(1, hd).astype(F32))


def _gated_deltanet(h, w_in, conv_w, a_log, dt_bias, norm_w, w_out, batch, seq):
    w_main = w_in[:, :GDN_MAIN_DIM].astype(BF16)
    w_ba = jnp.pad(w_in[:, GDN_MAIN_DIM:], ((0, 0), (0, LANES - 2 * GDN_V_HEADS))).astype(BF16)
    proj = _matmul(h, w_main, out_dtype=F32, tm=1024, tn=1024, name="gdn_in_proj")
    ba = _matmul(h, w_ba, out_dtype=F32, tm=1024, tn=LANES, name="gdn_ba_proj")
    qk = _gdn_conv(proj, conv_w, batch, seq, 0, 2 * GDN_KEY_DIM, True)
    vc = _gdn_conv(proj, conv_w, batch, seq, 2 * GDN_KEY_DIM, GDN_VALUE_DIM, False)
    gb, gbt = _gdn_gates(ba, a_log, dt_bias, batch, seq)
    o = _gdn_core(qk, vc, proj, gb, gbt, norm_w, batch, seq)
    return _matmul(o, w_out.astype(BF16), out_dtype=F32, tm=512, tn=1024, name="gdn_out_proj")


def _adaln(c, w, b):
    m = _matmul(c, w.astype(BF16), out_dtype=F32, tm=c.shape[0], tn=1024, bias=b, silu_a=True, name="adaln")
    d = w.shape[0]
    return m[:, :d], m[:, d:2 * d], m[:, 2 * d:]


def _moe_sublayer(x, h, top_idx, gate_w, gate_c, lng, lnb, seq, w_gu, b_gu, w_d, b_d, nxt):
    dest, slot_tok, tile_expert, n_valid = _moe_plan(top_idx[:, :MOE_TOPK])
    x_sorted = _gather_rows(h, slot_tok)
    y_sorted = _moe_experts(x_sorted, tile_expert, n_valid, w_gu.astype(BF16),
                            b_gu[:, None, :], w_d.astype(BF16), b_d[:, None, :])
    y4 = _gather_rows(y_sorted, dest)
    return _post(x, y4, gate_c, lng, lnb, seq, w4=gate_w, nxt=nxt)


def kernel(x, c, positions, ada_w, ada_b, ln_g, ln_b, moba_w_qkv, moba_w_o, gdn_w_in, gdn_conv_w, gdn_a_log,
           gdn_dt_bias, gdn_norm_w, gdn_w_out, router_w, router_b, moe_w_gate_up, moe_b_gate_up, moe_w_down,
           moe_b_down):
    batch, seq, d = x.shape
    t = batch * seq
    xs = x.reshape(t, d)
    cos_f, sin_s = _rope_tables(positions)
    mods = [[_adaln(c, ada_w[i, j], ada_b[i, j]) for j in range(2)] for i in range(DEPTH)]

    def router(i):
        wr = jnp.pad(router_w[i], ((0, 0), (0, LANES - N_EXPERTS)))
        br = jnp.concatenate([router_b[i], jnp.full((LANES - N_EXPERTS,), NEG_INF, F32)]).reshape(1, LANES)
        return wr, br

    for i in range(DEPTH):
        shift, scale, gate = mods[i][0]
        if i == 0:
            h = _modulate(xs, scale, shift, seq, BF16)
        if i % 2 == 0:
            qkv = _matmul(h, moba_w_qkv[i // 2].astype(BF16), out_dtype=F32, tm=1024, tn=1024, name="moba_qkv")
            o = _moba_attention(qkv, cos_f, sin_s, batch, seq)
            y = _matmul(o, moba_w_o[i // 2].astype(BF16), out_dtype=F32, tm=1024, tn=1024, name="moba_out")
        else:
            j = i // 2
            y = _gated_deltanet(h, gdn_w_in[j], gdn_conv_w[j], gdn_a_log[j], gdn_dt_bias[j], gdn_norm_w[j],
                                gdn_w_out[j], batch, seq)
        shift2, scale2, gate2 = mods[i][1]
        xs, h32, top_idx, gate_w = _post(xs, y, gate, ln_g[i, 0], ln_b[i, 0], seq, nxt=(scale2, shift2),
                                         router=router(i), h_tiled=True)
        nxt = None
        if i + 1 < DEPTH:
            shift_n, scale_n, _ = mods[i + 1][0]
            nxt = (scale_n, shift_n)
        res = _moe_sublayer(xs, h32, top_idx, gate_w, gate2, ln_g[i, 1], ln_b[i, 1], seq, moe_w_gate_up[i],
                            moe_b_gate_up[i], moe_w_down[i], moe_b_down[i], nxt)
        xs = res[0]
        if nxt is not None:
            h = res[1]
    return xs.reshape(batch, seq, d)
```

```python
import functools
import math

import jax
import jax.numpy as jnp
from jax import lax
from jax.experimental import pallas as pl
from jax.experimental.pallas import tpu as pltpu

F32 = jnp.float32
BF16 = jnp.bfloat16

D_MODEL = 2048
DEPTH = 2
DEEPNORM_ALPHA = float((2 * DEPTH) ** 0.25)
LN_EPS = 1e-5
MOBA_HEADS = 16
HEAD_DIM = 128
MOBA_BLOCK = 256
MOBA_TOPK = 3
ROPE_THETA = 500000.0
ROPE_DIM = HEAD_DIM // 4
NEG_INF = -1e30
GDN_K_HEADS = 16
GDN_V_HEADS = 32
GDN_KEY_DIM = GDN_K_HEADS * HEAD_DIM
GDN_VALUE_DIM = GDN_V_HEADS * HEAD_DIM
GDN_CONV_DIM = 2 * GDN_KEY_DIM + GDN_VALUE_DIM
GDN_MAIN_DIM = GDN_CONV_DIM + GDN_VALUE_DIM
GDN_CONV = 4
GDN_CHUNK = 128
GDN_NORM_EPS = 1e-6
L2_EPS = 1e-6
N_EXPERTS = 32
MOE_TOPK = 4
SWIGLU_ALPHA = 1.702
SWIGLU_LIMIT = 7.0

LANES = 128
VMEM_LIMIT = 50 * 1024 * 1024

MOE_TM = 512
MOE_TN = 1024
GATHER_ROWS = 1024
POST_TM = 256


def _cparams(sem):
    return pltpu.CompilerParams(dimension_semantics=sem, vmem_limit_bytes=VMEM_LIMIT)


def _mm_kernel(*refs, silu_a, has_bias):
    if has_bias:
        a_ref, b_ref, bias_ref, o_ref = refs
    else:
        a_ref, b_ref, o_ref = refs
    a = a_ref[...]
    if silu_a:
        a = a.astype(F32)
        a = a * jax.nn.sigmoid(a)
    acc = jnp.dot(a.astype(BF16), b_ref[...], preferred_element_type=F32)
    if has_bias:
        acc = acc + bias_ref[...]
    o_ref[...] = acc.astype(o_ref.dtype)


def _matmul(a, b, *, out_dtype, tm, tn, bias=None, silu_a=False, name="mm"):
    m, k = a.shape
    n = b.shape[1]
    tm = min(tm, m)
    tn = min(tn, n)
    assert m % tm == 0 and n % tn == 0, (m, n, tm, tn)
    in_specs = [pl.BlockSpec((tm, k), lambda i, j: (i, 0)),
                pl.BlockSpec((k, tn), lambda i, j: (0, j))]
    args = [a, b]
    if bias is not None:
        in_specs.append(pl.BlockSpec((1, tn), lambda i, j: (0, j)))
        args.append(bias.reshape(1, n).astype(F32))
    return pl.pallas_call(
        functools.partial(_mm_kernel, silu_a=silu_a, has_bias=bias is not None),
        grid=(m // tm, n // tn),
        in_specs=in_specs,
        out_specs=pl.BlockSpec((tm, tn), lambda i, j: (i, j)),
        out_shape=jax.ShapeDtypeStruct((m, n), out_dtype),
        compiler_params=_cparams(("parallel", "parallel")),
        name=name,
    )(*args)


def _modulate_kernel(x_ref, scale_ref, shift_ref, h_ref):
    h = x_ref[...] * (1.0 + scale_ref[0]) + shift_ref[0]
    h_ref[...] = h.astype(h_ref.dtype)


def _modulate(x, scale, shift, seq, out_dtype):
    t, d = x.shape
    tm = POST_TM
    per_b = seq // tm
    vec = pl.BlockSpec((1, 1, d), lambda i: (i // per_b, 0, 0))
    return pl.pallas_call(
        _modulate_kernel,
        grid=(t // tm,),
        in_specs=[pl.BlockSpec((tm, d), lambda i: (i, 0)), vec, vec],
        out_specs=pl.BlockSpec((tm, d), lambda i: (i, 0)),
        out_shape=jax.ShapeDtypeStruct((t, d), out_dtype),
        compiler_params=_cparams(("parallel",)),
        name="modulate",
    )(x, scale[:, None, :], shift[:, None, :])


def _top4(logits):
    lane = lax.broadcasted_iota(jnp.int32, logits.shape, 1).astype(F32)
    cur = logits
    vals, idxs = [], []
    for _ in range(MOE_TOPK):
        m = jnp.max(cur, axis=1, keepdims=True)
        idx = jnp.min(jnp.where(cur == m, lane, float(LANES)), axis=1, keepdims=True)
        vals.append(m)
        idxs.append(idx)
        cur = jnp.where(lane == idx, -3.0e38, cur)
    exps = [jnp.exp(v - vals[0]) for v in vals]
    denom = exps[0] + exps[1] + exps[2] + exps[3]
    idx_out = jnp.zeros_like(logits)
    gate_out = jnp.zeros_like(logits)
    for k in range(MOE_TOPK):
        idx_out = jnp.where(lane == float(k), idxs[k], idx_out)
        gate_out = jnp.where(lane == float(k), exps[k] / denom, gate_out)
    return idx_out.astype(jnp.int32), gate_out


def _store_tiled(ref, val, group=1, member=0):
    n, d = val.shape
    s_per = d // LANES
    for s in range(s_per):
        ref[pl.ds(member * s_per + s, n, stride=group * s_per), :] = val[:, s * LANES:(s + 1) * LANES]


def _load_tiled(ref, member, group, d=D_MODEL):
    s_per = d // LANES
    n = ref.shape[0] // (group * s_per)
    return jnp.concatenate([ref[pl.ds(member * s_per + s, n, stride=group * s_per), :] for s in range(s_per)],
                           axis=1)


def _post_kernel(*refs, moe_in, has_next, has_router, h_tiled, d):
    it = iter(refs)
    x_ref = next(it)
    y_ref = next(it)
    w4_ref = next(it) if moe_in else None
    gate_ref = next(it)
    lng_ref = next(it)
    lnb_ref = next(it)
    if has_next:
        scale_ref = next(it)
        shift_ref = next(it)
    if has_router:
        wr_ref = next(it)
        br_ref = next(it)
    xo_ref = next(it)
    if has_next:
        h_ref = next(it)
    if has_router:
        idx_ref = next(it)
        gw_ref = next(it)

    if moe_in:
        w4 = w4_ref[...]
        y = w4[:, 0:1] * _load_tiled(y_ref, 0, MOE_TOPK)
        for k in range(1, MOE_TOPK):
            y = y + w4[:, k:k + 1] * _load_tiled(y_ref, k, MOE_TOPK)
    else:
        y = y_ref[...].astype(F32)
    z = DEEPNORM_ALPHA * x_ref[...] + (1.0 + gate_ref[0]) * y
    mu = jnp.mean(z, axis=1, keepdims=True)
    zc = z - mu
    var = jnp.mean(zc * zc, axis=1, keepdims=True)
    xn = zc * lax.rsqrt(var + LN_EPS) * lng_ref[...] + lnb_ref[...]
    xo_ref[...] = xn
    if has_next:
        h = xn * (1.0 + scale_ref[0]) + shift_ref[0]
        if h_tiled:
            _store_tiled(h_ref, h)
        else:
            h_ref[...] = h.astype(h_ref.dtype)
    if has_router:
        logits = jnp.dot(h, wr_ref[...], precision=lax.Precision.HIGHEST,
                         preferred_element_type=F32) + br_ref[...]
        idx, gw = _top4(logits)
        idx_ref[...] = idx
        gw_ref[...] = gw


def _post(x, y, gate, lng, lnb, seq, *, w4=None, nxt=None, router=None, h_tiled=False):
    t, d = x.shape
    tm = POST_TM
    per_b = seq // tm
    s_per = d // LANES
    row = pl.BlockSpec((tm, d), lambda i: (i, 0))
    vec = pl.BlockSpec((1, 1, d), lambda i: (i // per_b, 0, 0))
    full = lambda a: pl.BlockSpec(a.shape, lambda i: (0,) * a.ndim)
    moe_in = w4 is not None
    args = [x, y]
    if moe_in:
        in_specs = [row, pl.BlockSpec((tm * MOE_TOPK * s_per, LANES), lambda i: (i, 0))]
    else:
        in_specs = [row, row]
    if moe_in:
        args.append(w4)
        in_specs.append(pl.BlockSpec((tm, LANES), lambda i: (i, 0)))
    lng2, lnb2 = lng.reshape(1, d), lnb.reshape(1, d)
    args += [gate[:, None, :], lng2, lnb2]
    in_specs += [vec, full(lng2), full(lnb2)]
    out_shape = [jax.ShapeDtypeStruct((t, d), F32)]
    out_specs = [row]
    if nxt is not None:
        args += [nxt[0][:, None, :], nxt[1][:, None, :]]
        in_specs += [vec, vec]
        if h_tiled:
            out_shape.append(jax.ShapeDtypeStruct((t * s_per, LANES), F32))
            out_specs.append(pl.BlockSpec((tm * s_per, LANES), lambda i: (i, 0)))
        else:
            out_shape.append(jax.ShapeDtypeStruct((t, d), BF16))
            out_specs.append(row)
    if router is not None:
        wr, br = router
        args += [wr, br]
        in_specs += [full(wr), full(br)]
        lane_row = pl.BlockSpec((tm, LANES), lambda i: (i, 0))
        out_shape += [jax.ShapeDtypeStruct((t, LANES), jnp.int32), jax.ShapeDtypeStruct((t, LANES), F32)]
        out_specs += [lane_row, lane_row]
    return pl.pallas_call(
        functools.partial(_post_kernel, moe_in=moe_in, has_next=nxt is not None,
                          has_router=router is not None, h_tiled=h_tiled, d=d),
        grid=(t // tm,),
        in_specs=in_specs,
        out_specs=out_specs,
        out_shape=out_shape,
        compiler_params=_cparams(("parallel",)),
        name="deepnorm_post",
    )(*args)


def _gather_kernel(idx_ref, src_ref, out_ref, sem, *, rows, s_per):
    def issue(r, carry):
        tok = idx_ref[0, 0, r]
        pltpu.make_async_copy(src_ref.at[pl.ds(tok * s_per, s_per), :],
                              out_ref.at[pl.ds(r * s_per, s_per), :], sem).start()
        return carry

    lax.fori_loop(0, rows, issue, 0, unroll=8)
    pltpu.make_async_copy(src_ref.at[pl.ds(0, rows * s_per), :], out_ref, sem).wait()


def _gather_rows(src, idx, d=D_MODEL, rows=GATHER_ROWS):
    n = idx.shape[0]
    rows = min(rows, n)
    assert n % rows == 0
    s_per = d // LANES
    return pl.pallas_call(
        functools.partial(_gather_kernel, rows=rows, s_per=s_per),
        grid=(n // rows,),
        in_specs=[pl.BlockSpec((1, 1, rows), lambda i: (i, 0, 0), memory_space=pltpu.SMEM),
                  pl.BlockSpec(memory_space=pl.ANY)],
        out_specs=pl.BlockSpec((rows * s_per, LANES), lambda i: (i, 0)),
        out_shape=jax.ShapeDtypeStruct((n * s_per, LANES), src.dtype),
        scratch_shapes=[pltpu.SemaphoreType.DMA],
        compiler_params=_cparams(("parallel",)),
        name="row_gather",
    )(idx.reshape(n // rows, 1, rows), src)


def _moe_gu_kernel(te_ref, nv_ref, x_ref, wg_ref, wu_ref, bg_ref, bu_ref, o_ref):
    i = pl.program_id(1)

    @pl.when(i < nv_ref[0])
    def _():
        x = _load_tiled(x_ref, 0, 1).astype(BF16)
        g = jnp.dot(x, wg_ref[0], preferred_element_type=F32) + bg_ref[0]
        u = jnp.dot(x, wu_ref[0], preferred_element_type=F32) + bu_ref[0]
        g = jnp.minimum(g, SWIGLU_LIMIT)
        u = jnp.clip(u, -SWIGLU_LIMIT, SWIGLU_LIMIT)
        o_ref[...] = (g * jax.nn.sigmoid(SWIGLU_ALPHA * g) * (u + 1.0)).astype(o_ref.dtype)

    @pl.when(i >= nv_ref[0])
    def _():
        o_ref[...] = jnp.zeros_like(o_ref)


def _moe_down_kernel(te_ref, nv_ref, a_ref, wd_ref, bd_ref, o_ref):
    i = pl.program_id(0)

    @pl.when(i < nv_ref[0])
    def _():
        _store_tiled(o_ref, jnp.dot(a_ref[...], wd_ref[0], preferred_element_type=F32) + bd_ref[0])

    @pl.when(i >= nv_ref[0])
    def _():
        o_ref[...] = jnp.zeros_like(o_ref)


def _moe_experts(x_sorted, tile_expert, n_valid, w_gu, b_gu, w_d, b_d):
    f, d = w_d.shape[1], w_d.shape[2]
    s_per = d // LANES
    n_slots = x_sorted.shape[0] // s_per
    tm, tn = MOE_TM, MOE_TN
    n_tiles = n_slots // tm
    nj = f // tn
    act = pl.pallas_call(
        _moe_gu_kernel,
        grid_spec=pltpu.PrefetchScalarGridSpec(
            num_scalar_prefetch=2,
            grid=(nj, n_tiles),
            in_specs=[pl.BlockSpec((tm * s_per, LANES), lambda j, i, te, nv: (i, 0)),
                      pl.BlockSpec((1, d, tn), lambda j, i, te, nv: (te[i], 0, j)),
                      pl.BlockSpec((1, d, tn), lambda j, i, te, nv: (te[i], 0, nj + j)),
                      pl.BlockSpec((1, 1, tn), lambda j, i, te, nv: (te[i], 0, j)),
                      pl.BlockSpec((1, 1, tn), lambda j, i, te, nv: (te[i], 0, nj + j))],
            out_specs=pl.BlockSpec((tm, tn), lambda j, i, te, nv: (i, j)),
        ),
        out_shape=jax.ShapeDtypeStruct((n_slots, f), BF16),
        compiler_params=_cparams(("arbitrary", "arbitrary")),
        name="moe_gate_up",
    )(tile_expert, n_valid, x_sorted, w_gu, w_gu, b_gu, b_gu)
    y = pl.pallas_call(
        _moe_down_kernel,
        grid_spec=pltpu.PrefetchScalarGridSpec(
            num_scalar_prefetch=2,
            grid=(n_tiles,),
            in_specs=[pl.BlockSpec((tm, f), lambda i, te, nv: (i, 0)),
                      pl.BlockSpec((1, f, d), lambda i, te, nv: (te[i], 0, 0)),
                      pl.BlockSpec((1, 1, d), lambda i, te, nv: (te[i], 0, 0))],
            out_specs=pl.BlockSpec((tm * s_per, LANES), lambda i, te, nv: (i, 0)),
        ),
        out_shape=jax.ShapeDtypeStruct((n_slots * s_per, LANES), F32),
        compiler_params=_cparams(("arbitrary",)),
        name="moe_down",
    )(tile_expert, n_valid, act, w_d, b_d)
    return y


def _moe_plan(top_idx):
    t = top_idx.shape[0]
    n_items = t * MOE_TOPK
    n_slots = n_items + N_EXPERTS * MOE_TM
    items = top_idx.reshape(n_items)
    onehot = (items[:, None] == jnp.arange(N_EXPERTS, dtype=jnp.int32)[None, :]).astype(jnp.int32)
    csum = jnp.cumsum(onehot, axis=0)
    counts = csum[-1]
    rank = jnp.sum(csum * onehot, axis=1) - 1
    padded = (counts + MOE_TM - 1) // MOE_TM * MOE_TM
    pad_end = jnp.cumsum(padded)
    pad_start = pad_end - padded
    dest = (pad_start[items] + rank).astype(jnp.int32)
    slot_tok = jnp.zeros((n_slots,), jnp.int32).at[dest].set(jnp.arange(n_items, dtype=jnp.int32) // MOE_TOPK)
    tile_start = jnp.arange(n_slots // MOE_TM, dtype=jnp.int32) * MOE_TM
    tile_expert = jnp.minimum(jnp.searchsorted(pad_end, tile_start, side='right'), N_EXPERTS - 1).astype(jnp.int32)
    n_valid = (pad_end[-1] // MOE_TM).astype(jnp.int32).reshape(1)
    return dest, slot_tok, tile_expert, n_valid


def _rope(x, cos_f, sin_s):
    lane = lax.broadcasted_iota(jnp.int32, x.shape, 1)
    half = ROPE_DIM // 2
    partner = jnp.where(lane < half, pltpu.roll(x, LANES - half, 1), pltpu.roll(x, half, 1))
    return x * cos_f + partner * sin_s


def _moba_kernel(q_ref, k_ref, v_ref, cos_ref, sin_ref, o_ref, *, n_blk):
    blk = MOBA_BLOCK
    cos_f, sin_s = cos_ref[...], sin_ref[...]
    kr = _rope(k_ref[...], cos_f, sin_s)
    means = [jnp.mean(kr[j * blk:(j + 1) * blk], axis=0, keepdims=True) for j in range(n_blk)]
    n_rows = max(n_blk, 8)
    if n_rows > n_blk:
        means.append(jnp.zeros((n_rows - n_blk, HEAD_DIM), F32))
    kmean = jnp.concatenate(means, axis=0).astype(BF16)
    ks = (kr * (HEAD_DIM ** -0.5)).astype(BF16)
    vt = v_ref[...].T.astype(BF16)
    qt = _rope(q_ref[...], cos_f, sin_s).T.astype(BF16)

    gate = jnp.dot(kmean, qt, preferred_element_type=F32)
    row_i = lax.broadcasted_iota(jnp.int32, gate.shape, 0)
    rowf = row_i.astype(F32)
    past = row_i < lax.broadcasted_iota(jnp.int32, gate.shape, 1) // blk
    cur = jnp.where(past, gate, NEG_INF)
    sel = jnp.zeros_like(gate)
    for _ in range(min(MOBA_TOPK, n_blk - 1)):
        m = jnp.max(cur, axis=0, keepdims=True)
        idx = jnp.min(jnp.where(cur == m, rowf, float(LANES)), axis=0, keepdims=True)
        hit = rowf == idx
        sel = jnp.where(hit & past, 1.0, sel)
        cur = jnp.where(hit, -3.0e38, cur)

    r_i = lax.broadcasted_iota(jnp.int32, (blk, blk), 0)
    c_i = lax.broadcasted_iota(jnp.int32, (blk, blk), 1)
    outs = []
    for qb in range(n_blk):
        cols = slice(qb * blk, (qb + 1) * blk)
        nk = (qb + 1) * blk
        s = jnp.dot(ks[:nk], qt[:, cols], preferred_element_type=F32)
        parts = [jnp.where(sel[j:j + 1, cols] > 0.0, s[j * blk:(j + 1) * blk], NEG_INF) for j in range(qb)]
        parts.append(jnp.where(r_i <= c_i, s[qb * blk:], NEG_INF))
        s = jnp.concatenate(parts, axis=0)
        p = jnp.exp(s - jnp.max(s, axis=0, keepdims=True))
        l = jnp.sum(p, axis=0, keepdims=True)
        acc = jnp.dot(vt[:, :nk], p.astype(BF16), preferred_element_type=F32)
        outs.append(acc / l)
    o_ref[...] = jnp.concatenate(outs, axis=1).T.astype(o_ref.dtype)


def _moba_attention(qkv, cos_f, sin_s, batch, seq):
    t = qkv.shape[0]
    n_blk = seq // MOBA_BLOCK
    h = MOBA_HEADS
    head = lambda off: pl.BlockSpec((seq, HEAD_DIM), lambda b, hh: (b, off + hh))
    table = pl.BlockSpec((seq, HEAD_DIM), lambda b, hh: (b, 0))
    return pl.pallas_call(
        functools.partial(_moba_kernel, n_blk=n_blk),
        grid=(batch, h),
        in_specs=[head(0), head(h), head(2 * h), table, table],
        out_specs=head(0),
        out_shape=jax.ShapeDtypeStruct((t, h * HEAD_DIM), BF16),
        compiler_params=_cparams(("parallel", "parallel")),
        name="moba_attention",
    )(qkv, qkv, qkv, cos_f, sin_s)


def _rope_tables(positions):
    half = ROPE_DIM // 2
    inv_freq = ROPE_THETA ** (-jnp.arange(0, ROPE_DIM, 2, dtype=F32) / ROPE_DIM)
    ang = positions.astype(F32).reshape(-1, 1) * inv_freq
    cos, sin = jnp.cos(ang), jnp.sin(ang)
    t = ang.shape[0]
    rest = HEAD_DIM - ROPE_DIM
    cos_f = jnp.concatenate([cos, cos, jnp.ones((t, rest), F32)], axis=1)
    sin_s = jnp.concatenate([-sin, sin, jnp.zeros((t, rest), F32)], axis=1)
    return cos_f, sin_s


def _shift_rows(x, k):
    row = lax.broadcasted_iota(jnp.int32, x.shape, 0)
    return jnp.where(row >= k, pltpu.roll(x, k, 0), 0.0)


def _gdn_conv_kernel(x_ref, w_ref, o_ref, *, l2norm):
    x = x_ref[...]
    w = w_ref[...]
    y = x * w[GDN_CONV - 1:GDN_CONV, :]
    for j in range(GDN_CONV - 1):
        y = y + _shift_rows(x, GDN_CONV - 1 - j) * w[j:j + 1, :]
    y = y * jax.nn.sigmoid(y)
    if l2norm:
        for hh in range(y.shape[1] // HEAD_DIM):
            ys = y[:, hh * HEAD_DIM:(hh + 1) * HEAD_DIM]
            ss = jnp.sum(ys * ys, axis=1, keepdims=True)
            o_ref[:, hh * HEAD_DIM:(hh + 1) * HEAD_DIM] = ys * lax.rsqrt(ss + L2_EPS)
    else:
        o_ref[...] = y


def _gdn_conv(proj, conv_w, batch, seq, col0, ncols, l2norm):
    tc = 256
    c0 = col0 // tc
    return pl.pallas_call(
        functools.partial(_gdn_conv_kernel, l2norm=l2norm),
        grid=(batch, ncols // tc),
        in_specs=[pl.BlockSpec((seq, tc), lambda b, c: (b, c0 + c)),
                  pl.BlockSpec((GDN_CONV, tc), lambda b, c: (0, c0 + c))],
        out_specs=pl.BlockSpec((seq, tc), lambda b, c: (b, c)),
        out_shape=jax.ShapeDtypeStruct((batch * seq, ncols), F32),
        compiler_params=_cparams(("parallel", "parallel")),
        name="gdn_conv",
    )(proj, conv_w)


def _gdn_gates_kernel(ba_ref, alog_ref, dtb_ref, gb_ref, gbt_ref):
    x = ba_ref[...]
    lane = lax.broadcasted_iota(jnp.int32, x.shape, 1)
    row = lax.broadcasted_iota(jnp.int32, x.shape, 0)
    xa = x + dtb_ref[...]
    softplus = jnp.maximum(xa, 0.0) + jnp.log(1.0 + jnp.exp(-jnp.abs(xa)))
    g = -jnp.exp(alog_ref[...]) * softplus
    pos = row % GDN_CHUNK
    sh = 1
    while sh < GDN_CHUNK:
        g = g + jnp.where(pos >= sh, pltpu.roll(g, sh, 0), 0.0)
        sh *= 2
    out = jnp.where(lane < GDN_V_HEADS, jax.nn.sigmoid(x), g)
    gb_ref[...] = out
    gbt_ref[...] = out.T


def _gdn_gates(ba, a_log, dt_bias, batch, seq):
    pad = jnp.zeros((LANES - 2 * GDN_V_HEADS,), F32)
    zeros = jnp.zeros((GDN_V_HEADS,), F32)
    alog_row = jnp.concatenate([zeros, a_log.astype(F32), pad]).reshape(1, LANES)
    dtb_row = jnp.concatenate([zeros, dt_bias.astype(F32), pad]).reshape(1, LANES)
    return pl.pallas_call(
        _gdn_gates_kernel,
        grid=(batch,),
        in_specs=[pl.BlockSpec((seq, LANES), lambda b: (b, 0)),
                  pl.BlockSpec((1, LANES), lambda b: (0, 0)),
                  pl.BlockSpec((1, LANES), lambda b: (0, 0))],
        out_specs=[pl.BlockSpec((seq, LANES), lambda b: (b, 0)),
                   pl.BlockSpec((LANES, seq), lambda b: (b, 0))],
        out_shape=[jax.ShapeDtypeStruct((batch * seq, LANES), F32),
                   jax.ShapeDtypeStruct((batch * LANES, seq), F32)],
        compiler_params=_cparams(("parallel",)),
        name="gdn_gates",
    )(ba, alog_row, dtb_row)


def _bmm(a, b):
    return jnp.einsum('nij,njk->nik', a.astype(BF16), b.astype(BF16), preferred_element_type=F32)


def _gdn_core_kernel(q_ref, k_ref, v_ref, z_ref, gb_ref, gbt_ref, nw_ref, o_ref,
                     b_s, c_s, qe_s, o0_s, eg_s, st_s, *, n_chunks):
    c = GDN_CHUNK
    hk = pl.program_id(1)
    q3 = (q_ref[...] * (HEAD_DIM ** -0.5)).reshape(n_chunks, c, HEAD_DIM)
    k2 = k_ref[...]
    k3 = k2.reshape(n_chunks, c, HEAD_DIM)
    kt = k2.T
    qh, kh = q3.astype(BF16), k3.astype(BF16)
    kk = jnp.einsum('nid,njd->nij', kh, kh, preferred_element_type=F32)
    qk = jnp.einsum('nid,njd->nij', qh, kh, preferred_element_type=F32)
    r_i = lax.broadcasted_iota(jnp.int32, (c, c), 0)
    c_i = lax.broadcasted_iota(jnp.int32, (c, c), 1)
    tril = c_i <= r_i
    strict = c_i < r_i
    eye = (c_i == r_i).astype(F32)
    lane = lax.broadcasted_iota(jnp.int32, gb_ref.shape, 1)
    gb = gb_ref[...]

    for vh in range(2):
        hv = 2 * hk + vh
        beta = jnp.sum(jnp.where(lane == hv, gb, 0.0), axis=1, keepdims=True).reshape(n_chunks, c, 1)
        gcol = jnp.sum(jnp.where(lane == GDN_V_HEADS + hv, gb, 0.0), axis=1, keepdims=True).reshape(n_chunks, c, 1)
        grow = gbt_ref[pl.ds(GDN_V_HEADS + hv, 1), :]
        decay, kdt = [], []
        for n in range(n_chunks):
            gr = grow[:, n * c:(n + 1) * c]
            diff = gcol[n] - gr
            decay.append(jnp.exp(jnp.where(tril, diff, NEG_INF)))
            kdt.append(kt[:, n * c:(n + 1) * c] * jnp.exp(gr[:, c - 1:c] - gr))
        decay = jnp.stack(decay, axis=0)
        kdt = jnp.stack(kdt, axis=0)
        m = jnp.where(strict, beta * kk * decay, 0.0)
        attn = jnp.where(tril, qk * decay, 0.0)
        t = eye - jnp.where((r_i // 2 == c_i // 2), m, 0.0)
        s = 2
        while s < c:
            off = (r_i // (2 * s) == c_i // (2 * s)) & ((r_i % (2 * s)) >= s) & ((c_i % (2 * s)) < s)
            moff = jnp.where(off, m, 0.0)
            t = t - _bmm(_bmm(t, moff), t)
            s *= 2
        eg = jnp.exp(gcol)
        v3 = v_ref[:, vh * HEAD_DIM:(vh + 1) * HEAD_DIM].reshape(n_chunks, c, HEAD_DIM)
        uw = _bmm(t, jnp.concatenate([v3 * beta, k3 * (beta * eg)], axis=-1))
        bc = _bmm(kdt, uw)
        ao = _bmm(attn, uw)
        b_s[vh] = bc[:, :, :HEAD_DIM]
        c_s[vh] = bc[:, :, HEAD_DIM:].astype(BF16)
        o0_s[vh] = ao[:, :, :HEAD_DIM]
        qe_s[vh] = (q3 * eg - ao[:, :, HEAD_DIM:]).astype(BF16)
        glast = gcol[:, c - 1:c, :]
        eg_s[vh] = jnp.broadcast_to(jnp.exp(glast), (n_chunks, 8, HEAD_DIM))
        st_s[vh] = jnp.zeros((HEAD_DIM, HEAD_DIM), F32)

    nw = nw_ref[...]

    def chunk_step(n, carry):
        rows = pl.ds(pl.multiple_of(n * c, c), c)
        for vh in range(2):
            state = st_s[vh]
            sb = state.astype(BF16)
            o = jnp.dot(qe_s[vh, n], sb, preferred_element_type=F32) + o0_s[vh, n]
            st_s[vh] = (state * eg_s[vh, n][0:1, :] + b_s[vh, n]
                        - jnp.dot(c_s[vh, n], sb, preferred_element_type=F32))
            zz = z_ref[rows, vh * HEAD_DIM:(vh + 1) * HEAD_DIM]
            on = o * lax.rsqrt(jnp.mean(o * o, axis=1, keepdims=True) + GDN_NORM_EPS)
            o_ref[rows, vh * HEAD_DIM:(vh + 1) * HEAD_DIM] = (on * nw * (zz * jax.nn.sigmoid(zz))).astype(o_ref.dtype)
        return carry

    lax.fori_loop(0, n_chunks, chunk_step, 0)


def _gdn_core(qk, vc, proj, gb, gbt, norm_w, batch, seq):
    n_chunks = seq // GDN_CHUNK
    hd = HEAD_DIM
    zblk0 = GDN_CONV_DIM // (2 * hd)
    big = lambda dt: pltpu.VMEM((2, n_chunks, GDN_CHUNK, hd), dt)
    return pl.pallas_call(
        functools.partial(_gdn_core_kernel, n_chunks=n_chunks),
        grid=(batch, GDN_K_HEADS),
        in_specs=[pl.BlockSpec((seq, hd), lambda b, h: (b, h)),
                  pl.BlockSpec((seq, hd), lambda b, h: (b, GDN_K_HEADS + h)),
                  pl.BlockSpec((seq, 2 * hd), lambda b, h: (b, h)),
                  pl.BlockSpec((seq, 2 * hd), lambda b, h: (b, zblk0 + h)),
                  pl.BlockSpec((seq, LANES), lambda b, h: (b, 0)),
                  pl.BlockSpec((LANES, seq), lambda b, h: (b, 0)),
                  pl.BlockSpec((1, hd), lambda b, h: (0, 0))],
        out_specs=pl.BlockSpec((seq, 2 * hd), lambda b, h: (b, h)),
        out_shape=jax.ShapeDtypeStruct((batch * seq, GDN_VALUE_DIM), BF16),
        scratch_shapes=[big(F32), big(BF16), big(BF16), big(F32),
                        pltpu.VMEM((2, n_chunks, 8, hd), F32), pltpu.VMEM((2, hd, hd), F32)],
        compiler_params=_cparams(("parallel", "arbitrary")),
        name="gdn_core",
    )(qk, qk, vc, proj, gb, gbt, norm_w.reshape(1, hd).astype(F32))


def _gated_deltanet(h, w_in, conv_w, a_log, dt_bias, norm_w, w_out, batch, seq):
    w_main = w_in[:, :GDN_MAIN_DIM].astype(BF16)
    w_ba = jnp.pad(w_in[:, GDN_MAIN_DIM:], ((0, 0), (0, LANES - 2 * GDN_V_HEADS))).astype(BF16)
    proj = _matmul(h, w_main, out_dtype=F32, tm=1024, tn=1024, name="gdn_in_proj")
    ba = _matmul(h, w_ba, out_dtype=F32, tm=1024, tn=LANES, name="gdn_ba_proj")
    qk = _gdn_conv(proj, conv_w, batch, seq, 0, 2 * GDN_KEY_DIM, True)
    vc = _gdn_conv(proj, conv_w, batch, seq, 2 * GDN_KEY_DIM, GDN_VALUE_DIM, False)
    gb, gbt = _gdn_gates(ba, a_log, dt_bias, batch, seq)
    o = _gdn_core(qk, vc, proj, gb, gbt, norm_w, batch, seq)
    return _matmul(o, w_out.astype(BF16), out_dtype=F32, tm=512, tn=1024, name="gdn_out_proj")


def _adaln(c, w, b):
    m = _matmul(c, w.astype(BF16), out_dtype=F32, tm=c.shape[0], tn=1024, bias=b, silu_a=True, name="adaln")
    d = w.shape[0]
    return m[:, :d], m[:, d:2 * d], m[:, 2 * d:]


def _moe_sublayer(x, h, top_idx, gate_w, gate_c, lng, lnb, seq, w_gu, b_gu, w_d, b_d, nxt):
    dest, slot_tok, tile_expert, n_valid = _moe_plan(top_idx[:, :MOE_TOPK])
    x_sorted = _gather_rows(h, slot_tok)
    y_sorted = _moe_experts(x_sorted, tile_expert, n_valid, w_gu.astype(BF16),
                            b_gu[:, None, :], w_d.astype(BF16), b_d[:, None, :])
    y4 = _gather_rows(y_sorted, dest)
    return _post(x, y4, gate_c, lng, lnb, seq, w4=gate_w, nxt=nxt)


def kernel(x, c, positions, ada_w, ada_b, ln_g, ln_b, moba_w_qkv, moba_w_o, gdn_w_in, gdn_conv_w, gdn_a_log,
           gdn_dt_bias, gdn_norm_w, gdn_w_out, router_w, router_b, moe_w_gate_up, moe_b_gate_up, moe_w_down,
           moe_b_down):
    batch, seq, d = x.shape
    t = batch * seq
    xs = x.reshape(t, d)
    cos_f, sin_s = _rope_tables(positions)
    mods = [[_adaln(c, ada_w[i, j], ada_b[i, j]) for j in range(2)] for i in range(DEPTH)]

    def router(i):
        wr = jnp.pad(router_w[i], ((0, 0), (0, LANES - N_EXPERTS)))
        br = jnp.concatenate([router_b[i], jnp.full((LANES - N_EXPERTS,), NEG_INF, F32)]).reshape(1, LANES)
        return wr, br

    for i in range(DEPTH):
        shift, scale, gate = mods[i][0]
        if i == 0:
            h = _modulate(xs, scale, shift, seq, BF16)
        if i % 2 == 0:
            qkv = _matmul(h, moba_w_qkv[i // 2].astype(BF16), out_dtype=F32, tm=1024, tn=1024, name="moba_qkv")
            o = _moba_attention(qkv, cos_f, sin_s, batch, seq)
            y = _matmul(o, moba_w_o[i // 2].astype(BF16), out_dtype=F32, tm=1024, tn=1024, name="moba_out")
        else:
            j = i // 2
            y = _gated_deltanet(h, gdn_w_in[j], gdn_conv_w[j], gdn_a_log[j], gdn_dt_bias[j], gdn_norm_w[j],
                                gdn_w_out[j], batch, seq)
        shift2, scale2, gate2 = mods[i][1]
        xs, h32, top_idx, gate_w = _post(xs, y, gate, ln_g[i, 0], ln_b[i, 0], seq, nxt=(scale2, shift2),
                                         router=router(i), h_tiled=True)
        nxt = None
        if i + 1 < DEPTH:
            shift_n, scale_n, _ = mods[i + 1][0]
            nxt = (scale_n, shift_n)
        res = _moe_sublayer(xs, h32, top_idx, gate_w, gate2, ln_g[i, 1], ln_b[i, 1], seq, moe_w_gate_up[i],
                            moe_b_gate_up[i], moe_w_down[i], moe_b_down[i], nxt)
        xs = res[0]
        if nxt is not None:
            h = res[1]
    return xs.reshape(batch, seq, d)
```

```python
import functools
import math

import jax
import jax.numpy as jnp
from jax import lax
from jax.experimental import pallas as pl
from jax.experimental.pallas import tpu as pltpu

F32 = jnp.float32
BF16 = jnp.bfloat16

D_MODEL = 2048
DEPTH = 2
DEEPNORM_ALPHA = float((2 * DEPTH) ** 0.25)
LN_EPS = 1e-5
MOBA_HEADS = 16
HEAD_DIM = 128
MOBA_BLOCK = 256
MOBA_TOPK = 3
ROPE_THETA = 500000.0
ROPE_DIM = HEAD_DIM // 4
NEG_INF = -1e30
GDN_K_HEADS = 16
GDN_V_HEADS = 32
GDN_KEY_DIM = GDN_K_HEADS * HEAD_DIM
GDN_VALUE_DIM = GDN_V_HEADS * HEAD_DIM
GDN_CONV_DIM = 2 * GDN_KEY_DIM + GDN_VALUE_DIM
GDN_MAIN_DIM = GDN_CONV_DIM + GDN_VALUE_DIM
GDN_CONV = 4
GDN_CHUNK = 128
GDN_NORM_EPS = 1e-6
L2_EPS = 1e-6
N_EXPERTS = 32
MOE_TOPK = 4
SWIGLU_ALPHA = 1.702
SWIGLU_LIMIT = 7.0

LANES = 128
VMEM_LIMIT = 50 * 1024 * 1024

MOE_TM = 512
MOE_TN = 1024
GATHER_ROWS = 1024
POST_TM = 256


def _cparams(sem):
    return pltpu.CompilerParams(dimension_semantics=sem, vmem_limit_bytes=VMEM_LIMIT)


def _mm_kernel(*refs, silu_a, has_bias):
    if has_bias:
        a_ref, b_ref, bias_ref, o_ref = refs
    else:
        a_ref, b_ref, o_ref = refs
    a = a_ref[...]
    if silu_a:
        a = a.astype(F32)
        a = a * jax.nn.sigmoid(a)
    acc = jnp.dot(a.astype(BF16), b_ref[...], preferred_element_type=F32)
    if has_bias:
        acc = acc + bias_ref[...]
    o_ref[...] = acc.astype(o_ref.dtype)


def _matmul(a, b, *, out_dtype, tm, tn, bias=None, silu_a=False, name="mm"):
    m, k = a.shape
    n = b.shape[1]
    tm = min(tm, m)
    tn = min(tn, n)
    assert m % tm == 0 and n % tn == 0, (m, n, tm, tn)
    in_specs = [pl.BlockSpec((tm, k), lambda i, j: (i, 0)),
                pl.BlockSpec((k, tn), lambda i, j: (0, j))]
    args = [a, b]
    if bias is not None:
        in_specs.append(pl.BlockSpec((1, tn), lambda i, j: (0, j)))
        args.append(bias.reshape(1, n).astype(F32))
    return pl.pallas_call(
        functools.partial(_mm_kernel, silu_a=silu_a, has_bias=bias is not None),
        grid=(m // tm, n // tn),
        in_specs=in_specs,
        out_specs=pl.BlockSpec((tm, tn), lambda i, j: (i, j)),
        out_shape=jax.ShapeDtypeStruct((m, n), out_dtype),
        compiler_params=_cparams(("parallel", "parallel")),
        name=name,
    )(*args)


def _modulate_kernel(x_ref, scale_ref, shift_ref, h_ref):
    h = x_ref[...] * (1.0 + scale_ref[0]) + shift_ref[0]
    h_ref[...] = h.astype(h_ref.dtype)


def _modulate(x, scale, shift, seq, out_dtype):
    t, d = x.shape
    tm = POST_TM
    per_b = seq // tm
    vec = pl.BlockSpec((1, 1, d), lambda i: (i // per_b, 0, 0))
    return pl.pallas_call(
        _modulate_kernel,
        grid=(t // tm,),
        in_specs=[pl.BlockSpec((tm, d), lambda i: (i, 0)), vec, vec],
        out_specs=pl.BlockSpec((tm, d), lambda i: (i, 0)),
        out_shape=jax.ShapeDtypeStruct((t, d), out_dtype),
        compiler_params=_cparams(("parallel",)),
        name="modulate",
    )(x, scale[:, None, :], shift[:, None, :])


def _top4(logits):
    lane = lax.broadcasted_iota(jnp.int32, logits.shape, 1).astype(F32)
    cur = logits
    vals, idxs = [], []
    for _ in range(MOE_TOPK):
        m = jnp.max(cur, axis=1, keepdims=True)
        idx = jnp.min(jnp.where(cur == m, lane, float(LANES)), axis=1, keepdims=True)
        vals.append(m)
        idxs.append(idx)
        cur = jnp.where(lane == idx, -3.0e38, cur)
    exps = [jnp.exp(v - vals[0]) for v in vals]
    denom = exps[0] + exps[1] + exps[2] + exps[3]
    idx_out = jnp.zeros_like(logits)
    gate_out = jnp.zeros_like(logits)
    for k in range(MOE_TOPK):
        idx_out = jnp.where(lane == float(k), idxs[k], idx_out)
        gate_out = jnp.where(lane == float(k), exps[k] / denom, gate_out)
    return idx_out.astype(jnp.int32), gate_out


def _store_tiled(ref, val, group=1, member=0):
    n, d = val.shape
    s_per = d // LANES
    for s in range(s_per):
        ref[pl.ds(member * s_per + s, n, stride=group * s_per), :] = val[:, s * LANES:(s + 1) * LANES]


def _load_tiled(ref, member, group, d=D_MODEL):
    s_per = d // LANES
    n = ref.shape[0] // (group * s_per)
    return jnp.concatenate([ref[pl.ds(member * s_per + s, n, stride=group * s_per), :] for s in range(s_per)],
                           axis=1)


def _post_kernel(*refs, moe_in, has_next, has_router, h_tiled, d):
    it = iter(refs)
    if moe_in:
        cur_idx_ref = next(it)
        nxt_idx_ref = next(it)
    x_ref = next(it)
    y_ref = next(it)
    w4_ref = next(it) if moe_in else None
    gate_ref = next(it)
    lng_ref = next(it)
    lnb_ref = next(it)
    if has_next:
        scale_ref = next(it)
        shift_ref = next(it)
    if has_router:
        wr_ref = next(it)
        br_ref = next(it)
    xo_ref = next(it)
    if has_next:
        h_ref = next(it)
    if has_router:
        idx_ref = next(it)
        gw_ref = next(it)
    if moe_in:
        ybuf = next(it)
        sems = next(it)

    if moe_in:
        i = pl.program_id(0)
        n_rows = x_ref.shape[0] * MOE_TOPK
        s_per = d // LANES

        def start_rows(src_idx_ref, slot):
            def issue(r, carry):
                row = src_idx_ref[0, 0, r]
                pltpu.make_async_copy(y_ref.at[pl.ds(row * s_per, s_per), :],
                                      ybuf.at[slot, pl.ds(r * s_per, s_per), :], sems.at[slot]).start()
                return carry
            lax.fori_loop(0, n_rows, issue, 0, unroll=8)

        @pl.when(i == 0)
        def _():
            start_rows(cur_idx_ref, 0)

        @pl.when(i + 1 < pl.num_programs(0))
        def _():
            start_rows(nxt_idx_ref, (i + 1) % 2)

        slot = i % 2
        pltpu.make_async_copy(y_ref.at[pl.ds(0, n_rows * s_per), :], ybuf.at[slot], sems.at[slot]).wait()
        yb = ybuf.at[slot]
        w4 = w4_ref[...]
        y = w4[:, 0:1] * _load_tiled(yb, 0, MOE_TOPK)
        for k in range(1, MOE_TOPK):
            y = y + w4[:, k:k + 1] * _load_tiled(yb, k, MOE_TOPK)
    else:
        y = y_ref[...].astype(F32)
    z = DEEPNORM_ALPHA * x_ref[...] + (1.0 + gate_ref[0]) * y
    mu = jnp.mean(z, axis=1, keepdims=True)
    zc = z - mu
    var = jnp.mean(zc * zc, axis=1, keepdims=True)
    xn = zc * lax.rsqrt(var + LN_EPS) * lng_ref[...] + lnb_ref[...]
    xo_ref[...] = xn
    if has_next:
        h = xn * (1.0 + scale_ref[0]) + shift_ref[0]
        if h_tiled:
            _store_tiled(h_ref, _pack_bf16_pairs(h))
        else:
            h_ref[...] = h.astype(h_ref.dtype)
    if has_router:
        logits = jnp.dot(h, wr_ref[...], precision=lax.Precision.HIGHEST,
                         preferred_element_type=F32) + br_ref[...]
        idx, gw = _top4(logits)
        idx_ref[...] = idx
        gw_ref[...] = gw


def _post(x, y, gate, lng, lnb, seq, *, w4=None, nxt=None, router=None, h_tiled=False):
    t, d = x.shape
    tm = POST_TM
    per_b = seq // tm
    s_per = d // LANES
    row = pl.BlockSpec((tm, d), lambda i: (i, 0))
    vec = pl.BlockSpec((1, 1, d), lambda i: (i // per_b, 0, 0))
    full = lambda a: pl.BlockSpec(a.shape, lambda i: (0,) * a.ndim)
    moe_in = w4 is not None
    n_steps = t // tm
    scratch = []
    if moe_in:
        w4, dest = w4
        n_rows = tm * MOE_TOPK
        dest3 = dest.reshape(n_steps, 1, n_rows)
        smem = lambda f: pl.BlockSpec((1, 1, n_rows), f, memory_space=pltpu.SMEM)
        args = [dest3, dest3, x, y, w4]
        in_specs = [smem(lambda i: (i, 0, 0)), smem(lambda i: (jnp.minimum(i + 1, n_steps - 1), 0, 0)),
                    row, pl.BlockSpec(memory_space=pl.ANY), pl.BlockSpec((tm, LANES), lambda i: (i, 0))]
        scratch = [pltpu.VMEM((2, n_rows * s_per, LANES), F32), pltpu.SemaphoreType.DMA((2,))]
    else:
        args = [x, y]
        in_specs = [row, row]
    lng2, lnb2 = lng.reshape(1, d), lnb.reshape(1, d)
    args += [gate[:, None, :], lng2, lnb2]
    in_specs += [vec, full(lng2), full(lnb2)]
    out_shape = [jax.ShapeDtypeStruct((t, d), F32)]
    out_specs = [row]
    if nxt is not None:
        args += [nxt[0][:, None, :], nxt[1][:, None, :]]
        in_specs += [vec, vec]
        if h_tiled:
            out_shape.append(jax.ShapeDtypeStruct((t * s_per // 2, LANES), jnp.uint32))
            out_specs.append(pl.BlockSpec((tm * s_per // 2, LANES), lambda i: (i, 0)))
        else:
            out_shape.append(jax.ShapeDtypeStruct((t, d), BF16))
            out_specs.append(row)
    if router is not None:
        wr, br = router
        args += [wr, br]
        in_specs += [full(wr), full(br)]
        lane_row = pl.BlockSpec((tm, LANES), lambda i: (i, 0))
        out_shape += [jax.ShapeDtypeStruct((t, LANES), jnp.int32), jax.ShapeDtypeStruct((t, LANES), F32)]
        out_specs += [lane_row, lane_row]
    return pl.pallas_call(
        functools.partial(_post_kernel, moe_in=moe_in, has_next=nxt is not None,
                          has_router=router is not None, h_tiled=h_tiled, d=d),
        grid=(n_steps,),
        in_specs=in_specs,
        out_specs=out_specs,
        out_shape=out_shape,
        scratch_shapes=scratch,
        compiler_params=_cparams(("arbitrary",) if moe_in else ("parallel",)),
        name="moe_combine_post" if moe_in else "deepnorm_post",
    )(*args)


def _gather_kernel(idx_ref, src_ref, out_ref, sem, *, rows, s_per):
    def issue(r, carry):
        tok = idx_ref[0, 0, r]
        pltpu.make_async_copy(src_ref.at[pl.ds(tok * s_per, s_per), :],
                              out_ref.at[pl.ds(r * s_per, s_per), :], sem).start()
        return carry

    lax.fori_loop(0, rows, issue, 0, unroll=8)
    pltpu.make_async_copy(src_ref.at[pl.ds(0, rows * s_per), :], out_ref, sem).wait()


def _gather_rows(src, idx, d=D_MODEL, rows=GATHER_ROWS):
    n = idx.shape[0]
    rows = min(rows, n)
    assert n % rows == 0
    s_per = d // LANES
    return pl.pallas_call(
        functools.partial(_gather_kernel, rows=rows, s_per=s_per),
        grid=(n // rows,),
        in_specs=[pl.BlockSpec((1, 1, rows), lambda i: (i, 0, 0), memory_space=pltpu.SMEM),
                  pl.BlockSpec(memory_space=pl.ANY)],
        out_specs=pl.BlockSpec((rows * s_per, LANES), lambda i: (i, 0)),
        out_shape=jax.ShapeDtypeStruct((n * s_per, LANES), src.dtype),
        scratch_shapes=[pltpu.SemaphoreType.DMA],
        compiler_params=_cparams(("parallel",)),
        name="row_gather",
    )(idx.reshape(n // rows, 1, rows), src)


def _pack_bf16_pairs(h):
    half = h.shape[1] // 2
    bits = lax.bitcast_convert_type(h, jnp.uint32)
    rounded = bits + jnp.uint32(0x7FFF) + ((bits >> 16) & jnp.uint32(1))
    return (rounded[:, half:] & jnp.uint32(0xFFFF0000)) | (rounded[:, :half] >> 16)


def _unpack_bf16_pairs(words):
    lo = lax.bitcast_convert_type(words << 16, F32).astype(BF16)
    hi = lax.bitcast_convert_type(words & jnp.uint32(0xFFFF0000), F32).astype(BF16)
    return jnp.concatenate([lo, hi], axis=1)


def _moe_gu_kernel(te_ref, nv_ref, x_ref, wg_ref, wu_ref, bg_ref, bu_ref, o_ref):
    i = pl.program_id(1)

    @pl.when(i < nv_ref[0])
    def _():
        x = _unpack_bf16_pairs(_load_tiled(x_ref, 0, 1, d=x_ref.shape[0] // MOE_TM * LANES))
        g = jnp.dot(x, wg_ref[0, 0], preferred_element_type=F32) + bg_ref[0, 0]
        u = jnp.dot(x, wu_ref[0, 0], preferred_element_type=F32) + bu_ref[0, 0]
        g = jnp.minimum(g, SWIGLU_LIMIT)
        u = jnp.clip(u, -SWIGLU_LIMIT, SWIGLU_LIMIT)
        o_ref[...] = (g * jax.nn.sigmoid(SWIGLU_ALPHA * g) * (u + 1.0)).astype(o_ref.dtype)

    @pl.when(i >= nv_ref[0])
    def _():
        o_ref[...] = jnp.zeros_like(o_ref)


def _moe_down_kernel(te_ref, nv_ref, a_ref, wd_ref, bd_ref, o_ref):
    i = pl.program_id(0)

    @pl.when(i < nv_ref[0])
    def _():
        _store_tiled(o_ref, jnp.dot(a_ref[...], wd_ref[0, 0], preferred_element_type=F32) + bd_ref[0, 0])

    @pl.when(i >= nv_ref[0])
    def _():
        o_ref[...] = jnp.zeros_like(o_ref)


def _moe_experts(x_sorted, tile_expert, n_valid, w_gu, b_gu, w_d, b_d, layer):
    f, d = w_d.shape[2], w_d.shape[3]
    s_per = d // LANES
    s_in = s_per // 2
    n_slots = x_sorted.shape[0] // s_in
    tm, tn = MOE_TM, MOE_TN
    n_tiles = n_slots // tm
    nj = f // tn
    act = pl.pallas_call(
        _moe_gu_kernel,
        grid_spec=pltpu.PrefetchScalarGridSpec(
            num_scalar_prefetch=2,
            grid=(nj, n_tiles),
            in_specs=[pl.BlockSpec((tm * s_in, LANES), lambda j, i, te, nv: (i, 0)),
                      pl.BlockSpec((1, 1, d, tn), lambda j, i, te, nv: (layer, te[i], 0, j)),
                      pl.BlockSpec((1, 1, d, tn), lambda j, i, te, nv: (layer, te[i], 0, nj + j)),
                      pl.BlockSpec((1, 1, 1, tn), lambda j, i, te, nv: (layer, te[i], 0, j)),
                      pl.BlockSpec((1, 1, 1, tn), lambda j, i, te, nv: (layer, te[i], 0, nj + j))],
            out_specs=pl.BlockSpec((tm, tn), lambda j, i, te, nv: (i, j)),
        ),
        out_shape=jax.ShapeDtypeStruct((n_slots, f), BF16),
        compiler_params=_cparams(("arbitrary", "arbitrary")),
        name="moe_gate_up",
    )(tile_expert, n_valid, x_sorted, w_gu, w_gu, b_gu, b_gu)
    y = pl.pallas_call(
        _moe_down_kernel,
        grid_spec=pltpu.PrefetchScalarGridSpec(
            num_scalar_prefetch=2,
            grid=(n_tiles,),
            in_specs=[pl.BlockSpec((tm, f), lambda i, te, nv: (i, 0)),
                      pl.BlockSpec((1, 1, f, d), lambda i, te, nv: (layer, te[i], 0, 0)),
                      pl.BlockSpec((1, 1, 1, d), lambda i, te, nv: (layer, te[i], 0, 0))],
            out_specs=pl.BlockSpec((tm * s_per, LANES), lambda i, te, nv: (i, 0)),
        ),
        out_shape=jax.ShapeDtypeStruct((n_slots * s_per, LANES), F32),
        compiler_params=_cparams(("arbitrary",)),
        name="moe_down",
    )(tile_expert, n_valid, act, w_d, b_d)
    return y


def _moe_plan(top_idx):
    t = top_idx.shape[0]
    n_items = t * MOE_TOPK
    n_slots = n_items + N_EXPERTS * MOE_TM
    items = top_idx.reshape(n_items)
    onehot = (items[:, None] == jnp.arange(N_EXPERTS, dtype=jnp.int32)[None, :]).astype(jnp.int32)
    csum = jnp.cumsum(onehot, axis=0)
    counts = csum[-1]
    rank = jnp.sum(csum * onehot, axis=1) - 1
    padded = (counts + MOE_TM - 1) // MOE_TM * MOE_TM
    pad_end = jnp.cumsum(padded)
    pad_start = pad_end - padded
    dest = (pad_start[items] + rank).astype(jnp.int32)
    slot_tok = (jnp.arange(n_slots, dtype=jnp.int32) % t).at[dest].set(
        jnp.arange(n_items, dtype=jnp.int32) // MOE_TOPK)
    tile_start = jnp.arange(n_slots // MOE_TM, dtype=jnp.int32) * MOE_TM
    tile_expert = jnp.minimum(jnp.searchsorted(pad_end, tile_start, side='right'), N_EXPERTS - 1).astype(jnp.int32)
    n_valid = (pad_end[-1] // MOE_TM).astype(jnp.int32).reshape(1)
    return dest, slot_tok, tile_expert, n_valid


def _rope(x, cos_f, sin_s):
    lane = lax.broadcasted_iota(jnp.int32, x.shape, 1)
    half = ROPE_DIM // 2
    partner = jnp.where(lane < half, pltpu.roll(x, LANES - half, 1), pltpu.roll(x, half, 1))
    return x * cos_f + partner * sin_s


def _moba_kernel(q_ref, k_ref, v_ref, cos_ref, sin_ref, o_ref, *, n_blk):
    blk = MOBA_BLOCK
    cos_f, sin_s = cos_ref[...], sin_ref[...]
    kr = _rope(k_ref[...], cos_f, sin_s)
    means = [jnp.mean(kr[j * blk:(j + 1) * blk], axis=0, keepdims=True) for j in range(n_blk)]
    n_rows = max(n_blk, 8)
    if n_rows > n_blk:
        means.append(jnp.zeros((n_rows - n_blk, HEAD_DIM), F32))
    kmean = jnp.concatenate(means, axis=0).astype(BF16)
    ks = (kr * (HEAD_DIM ** -0.5)).astype(BF16)
    vt = v_ref[...].T.astype(BF16)
    qt = _rope(q_ref[...], cos_f, sin_s).T.astype(BF16)

    gate = jnp.dot(kmean, qt, preferred_element_type=F32)
    row_i = lax.broadcasted_iota(jnp.int32, gate.shape, 0)
    rowf = row_i.astype(F32)
    past = row_i < lax.broadcasted_iota(jnp.int32, gate.shape, 1) // blk
    cur = jnp.where(past, gate, NEG_INF)
    sel = jnp.zeros_like(gate)
    for _ in range(min(MOBA_TOPK, n_blk - 1)):
        m = jnp.max(cur, axis=0, keepdims=True)
        idx = jnp.min(jnp.where(cur == m, rowf, float(LANES)), axis=0, keepdims=True)
        hit = rowf == idx
        sel = jnp.where(hit & past, 1.0, sel)
        cur = jnp.where(hit, -3.0e38, cur)

    r_i = lax.broadcasted_iota(jnp.int32, (blk, blk), 0)
    c_i = lax.broadcasted_iota(jnp.int32, (blk, blk), 1)
    outs = []
    for qb in range(n_blk):
        cols = slice(qb * blk, (qb + 1) * blk)
        nk = (qb + 1) * blk
        s = jnp.dot(ks[:nk], qt[:, cols], preferred_element_type=F32)
        parts = [jnp.where(sel[j:j + 1, cols] > 0.0, s[j * blk:(j + 1) * blk], NEG_INF) for j in range(qb)]
        parts.append(jnp.where(r_i <= c_i, s[qb * blk:], NEG_INF))
        s = jnp.concatenate(parts, axis=0)
        p = jnp.exp(s - jnp.max(s, axis=0, keepdims=True))
        l = jnp.sum(p, axis=0, keepdims=True)
        acc = jnp.dot(vt[:, :nk], p.astype(BF16), preferred_element_type=F32)
        outs.append(acc / l)
    o_ref[...] = jnp.concatenate(outs, axis=1).T.astype(o_ref.dtype)


def _moba_attention(qkv, cos_f, sin_s, batch, seq):
    t = qkv.shape[0]
    n_blk = seq // MOBA_BLOCK
    h = MOBA_HEADS
    head = lambda off: pl.BlockSpec((seq, HEAD_DIM), lambda b, hh: (b, off + hh))
    table = pl.BlockSpec((seq, HEAD_DIM), lambda b, hh: (b, 0))
    return pl.pallas_call(
        functools.partial(_moba_kernel, n_blk=n_blk),
        grid=(batch, h),
        in_specs=[head(0), head(h), head(2 * h), table, table],
        out_specs=head(0),
        out_shape=jax.ShapeDtypeStruct((t, h * HEAD_DIM), BF16),
        compiler_params=_cparams(("parallel", "parallel")),
        name="moba_attention",
    )(qkv, qkv, qkv, cos_f, sin_s)


def _rope_tables(positions):
    half = ROPE_DIM // 2
    inv_freq = ROPE_THETA ** (-jnp.arange(0, ROPE_DIM, 2, dtype=F32) / ROPE_DIM)
    ang = positions.astype(F32).reshape(-1, 1) * inv_freq
    cos, sin = jnp.cos(ang), jnp.sin(ang)
    t = ang.shape[0]
    rest = HEAD_DIM - ROPE_DIM
    cos_f = jnp.concatenate([cos, cos, jnp.ones((t, rest), F32)], axis=1)
    sin_s = jnp.concatenate([-sin, sin, jnp.zeros((t, rest), F32)], axis=1)
    return cos_f, sin_s


def _shift_rows(x, k):
    row = lax.broadcasted_iota(jnp.int32, x.shape, 0)
    return jnp.where(row >= k, pltpu.roll(x, k, 0), 0.0)


def _conv_silu(x, w):
    y = x * w[GDN_CONV - 1:GDN_CONV, :]
    for j in range(GDN_CONV - 1):
        y = y + _shift_rows(x, GDN_CONV - 1 - j) * w[j:j + 1, :]
    return y * jax.nn.sigmoid(y)


def _l2_normalize(y):
    return y * lax.rsqrt(jnp.sum(y * y, axis=1, keepdims=True) + L2_EPS)


def _gdn_gates_kernel(ba_ref, alog_ref, dtb_ref, gb_ref, gbt_ref):
    x = ba_ref[...]
    lane = lax.broadcasted_iota(jnp.int32, x.shape, 1)
    row = lax.broadcasted_iota(jnp.int32, x.shape, 0)
    xa = x + dtb_ref[...]
    softplus = jnp.maximum(xa, 0.0) + jnp.log(1.0 + jnp.exp(-jnp.abs(xa)))
    g = -jnp.exp(alog_ref[...]) * softplus
    pos = row % GDN_CHUNK
    sh = 1
    while sh < GDN_CHUNK:
        g = g + jnp.where(pos >= sh, pltpu.roll(g, sh, 0), 0.0)
        sh *= 2
    out = jnp.where(lane < GDN_V_HEADS, jax.nn.sigmoid(x), g)
    gb_ref[...] = out
    gbt_ref[...] = out.T


def _gdn_gates(ba, a_log, dt_bias, batch, seq):
    pad = jnp.zeros((LANES - 2 * GDN_V_HEADS,), F32)
    zeros = jnp.zeros((GDN_V_HEADS,), F32)
    alog_row = jnp.concatenate([zeros, a_log.astype(F32), pad]).reshape(1, LANES)
    dtb_row = jnp.concatenate([zeros, dt_bias.astype(F32), pad]).reshape(1, LANES)
    return pl.pallas_call(
        _gdn_gates_kernel,
        grid=(batch,),
        in_specs=[pl.BlockSpec((seq, LANES), lambda b: (b, 0)),
                  pl.BlockSpec((1, LANES), lambda b: (0, 0)),
                  pl.BlockSpec((1, LANES), lambda b: (0, 0))],
        out_specs=[pl.BlockSpec((seq, LANES), lambda b: (b, 0)),
                   pl.BlockSpec((LANES, seq), lambda b: (b, 0))],
        out_shape=[jax.ShapeDtypeStruct((batch * seq, LANES), F32),
                   jax.ShapeDtypeStruct((batch * LANES, seq), F32)],
        compiler_params=_cparams(("parallel",)),
        name="gdn_gates",
    )(ba, alog_row, dtb_row)


def _bmm(a, b):
    return jnp.einsum('nij,njk->nik', a.astype(BF16), b.astype(BF16), preferred_element_type=F32)


def _gdn_core_kernel(q_ref, k_ref, v_ref, z_ref, cwq_ref, cwk_ref, cwv_ref, gb_ref, gbt_ref, nw_ref, o_ref,
                     b_s, c_s, qe_s, o0_s, eg_s, st_s, *, n_chunks):
    c = GDN_CHUNK
    hk = pl.program_id(1)
    q2 = _l2_normalize(_conv_silu(q_ref[...], cwq_ref[...]))
    k2 = _l2_normalize(_conv_silu(k_ref[...], cwk_ref[...]))
    v2 = _conv_silu(v_ref[...], cwv_ref[...])
    q3 = (q2 * (HEAD_DIM ** -0.5)).reshape(n_chunks, c, HEAD_DIM)
    k3 = k2.reshape(n_chunks, c, HEAD_DIM)
    kt = k2.T
    qh, kh = q3.astype(BF16), k3.astype(BF16)
    kk = jnp.einsum('nid,njd->nij', kh, kh, preferred_element_type=F32)
    qk = jnp.einsum('nid,njd->nij', qh, kh, preferred_element_type=F32)
    r_i = lax.broadcasted_iota(jnp.int32, (c, c), 0)
    c_i = lax.broadcasted_iota(jnp.int32, (c, c), 1)
    tril = c_i <= r_i
    strict = c_i < r_i
    eye = (c_i == r_i).astype(F32)
    lane = lax.broadcasted_iota(jnp.int32, gb_ref.shape, 1)
    gb = gb_ref[...]

    for vh in range(2):
        hv = 2 * hk + vh
        beta = jnp.sum(jnp.where(lane == hv, gb, 0.0), axis=1, keepdims=True).reshape(n_chunks, c, 1)
        gcol = jnp.sum(jnp.where(lane == GDN_V_HEADS + hv, gb, 0.0), axis=1, keepdims=True).reshape(n_chunks, c, 1)
        grow = gbt_ref[pl.ds(GDN_V_HEADS + hv, 1), :]
        decay, kdt = [], []
        for n in range(n_chunks):
            gr = grow[:, n * c:(n + 1) * c]
            diff = gcol[n] - gr
            decay.append(jnp.exp(jnp.where(tril, diff, NEG_INF)))
            kdt.append(kt[:, n * c:(n + 1) * c] * jnp.exp(gr[:, c - 1:c] - gr))
        decay = jnp.stack(decay, axis=0)
        kdt = jnp.stack(kdt, axis=0)
        m = jnp.where(strict, beta * kk * decay, 0.0)
        attn = jnp.where(tril, qk * decay, 0.0)
        t = eye - jnp.where((r_i // 2 == c_i // 2), m, 0.0)
        s = 2
        while s < c:
            off = (r_i // (2 * s) == c_i // (2 * s)) & ((r_i % (2 * s)) >= s) & ((c_i % (2 * s)) < s)
            moff = jnp.where(off, m, 0.0)
            t = t - _bmm(_bmm(t, moff), t)
            s *= 2
        eg = jnp.exp(gcol)
        v3 = v2[:, vh * HEAD_DIM:(vh + 1) * HEAD_DIM].reshape(n_chunks, c, HEAD_DIM)
        uw = _bmm(t, jnp.concatenate([v3 * beta, k3 * (beta * eg)], axis=-1))
        bc = _bmm(kdt, uw)
        ao = _bmm(attn, uw)
        b_s[vh] = bc[:, :, :HEAD_DIM]
        c_s[vh] = bc[:, :, HEAD_DIM:].astype(BF16)
        o0_s[vh] = ao[:, :, :HEAD_DIM]
        qe_s[vh] = (q3 * eg - ao[:, :, HEAD_DIM:]).astype(BF16)
        glast = gcol[:, c - 1:c, :]
        eg_s[vh] = jnp.broadcast_to(jnp.exp(glast), (n_chunks, 8, HEAD_DIM))
        st_s[vh] = jnp.zeros((HEAD_DIM, HEAD_DIM), F32)

    nw = nw_ref[...]

    def chunk_step(n, carry):
        rows = pl.ds(pl.multiple_of(n * c, c), c)
        for vh in range(2):
            state = st_s[vh]
            sb = state.astype(BF16)
            o = jnp.dot(qe_s[vh, n], sb, preferred_element_type=F32) + o0_s[vh, n]
            st_s[vh] = (state * eg_s[vh, n][0:1, :] + b_s[vh, n]
                        - jnp.dot(c_s[vh, n], sb, preferred_element_type=F32))
            zz = z_ref[rows, vh * HEAD_DIM:(vh + 1) * HEAD_DIM]
            on = o * lax.rsqrt(jnp.mean(o * o, axis=1, keepdims=True) + GDN_NORM_EPS)
            o_ref[rows, vh * HEAD_DIM:(vh + 1) * HEAD_DIM] = (on * nw * (zz * jax.nn.sigmoid(zz))).astype(o_ref.dtype)
        return carry

    lax.fori_loop(0, n_chunks, chunk_step, 0)


def _gdn_core(proj, conv_w, gb, gbt, norm_w, batch, seq):
    n_chunks = seq // GDN_CHUNK
    hd = HEAD_DIM
    vblk0 = 2 * GDN_KEY_DIM // (2 * hd)
    zblk0 = GDN_CONV_DIM // (2 * hd)
    big = lambda dt: pltpu.VMEM((2, n_chunks, GDN_CHUNK, hd), dt)
    return pl.pallas_call(
        functools.partial(_gdn_core_kernel, n_chunks=n_chunks),
        grid=(batch, GDN_K_HEADS),
        in_specs=[pl.BlockSpec((seq, hd), lambda b, h: (b, h)),
                  pl.BlockSpec((seq, hd), lambda b, h: (b, GDN_K_HEADS + h)),
                  pl.BlockSpec((seq, 2 * hd), lambda b, h: (b, vblk0 + h)),
                  pl.BlockSpec((seq, 2 * hd), lambda b, h: (b, zblk0 + h)),
                  pl.BlockSpec((GDN_CONV, hd), lambda b, h: (0, h)),
                  pl.BlockSpec((GDN_CONV, hd), lambda b, h: (0, GDN_K_HEADS + h)),
                  pl.BlockSpec((GDN_CONV, 2 * hd), lambda b, h: (0, vblk0 + h)),
                  pl.BlockSpec((seq, LANES), lambda b, h: (b, 0)),
                  pl.BlockSpec((LANES, seq), lambda b, h: (b, 0)),
                  pl.BlockSpec((1, hd), lambda b, h: (0, 0))],
        out_specs=pl.BlockSpec((seq, 2 * hd), lambda b, h: (b, h)),
        out_shape=jax.ShapeDtypeStruct((batch * seq, GDN_VALUE_DIM), BF16),
        scratch_shapes=[big(F32), big(BF16), big(BF16), big(F32),
                        pltpu.VMEM((2, n_chunks, 8, hd), F32), pltpu.VMEM((2, hd, hd), F32)],
        compiler_params=_cparams(("parallel", "arbitrary")),
        name="gdn_core",
    )(proj, proj, proj, proj, conv_w, conv_w, conv_w, gb, gbt, norm_w.reshape(1, hd).astype(F32))


def _gated_deltanet(h, w_in, conv_w, a_log, dt_bias, norm_w, w_out, batch, seq):
    w_main = w_in[:, :GDN_MAIN_DIM].astype(BF16)
    w_ba = jnp.pad(w_in[:, GDN_MAIN_DIM:], ((0, 0), (0, LANES - 2 * GDN_V_HEADS))).astype(BF16)
    proj = _matmul(h, w_main, out_dtype=F32, tm=1024, tn=1024, name="gdn_in_proj")
    ba = _matmul(h, w_ba, out_dtype=F32, tm=1024, tn=LANES, name="gdn_ba_proj")
    gb, gbt = _gdn_gates(ba, a_log, dt_bias, batch, seq)
    o = _gdn_core(proj, conv_w, gb, gbt, norm_w, batch, seq)
    return _matmul(o, w_out.astype(BF16), out_dtype=F32, tm=512, tn=1024, name="gdn_out_proj")


def _adaln(c, w, b):
    m = _matmul(c, w.astype(BF16), out_dtype=F32, tm=c.shape[0], tn=1024, bias=b, silu_a=True, name="adaln")
    d = w.shape[0]
    return m[:, :d], m[:, d:2 * d], m[:, 2 * d:]


def _moe_sublayer(x, h, top_idx, gate_w, gate_c, lng, lnb, seq, moe_w, layer, nxt):
    dest, slot_tok, tile_expert, n_valid = _moe_plan(top_idx[:, :MOE_TOPK])
    x_sorted = _gather_rows(h, slot_tok, d=D_MODEL // 2)
    y_sorted = _moe_experts(x_sorted, tile_expert, n_valid, *moe_w, layer)
    return _post(x, y_sorted, gate_c, lng, lnb, seq, w4=(gate_w, dest), nxt=nxt)


def kernel(x, c, positions, ada_w, ada_b, ln_g, ln_b, moba_w_qkv, moba_w_o, gdn_w_in, gdn_conv_w, gdn_a_log,
           gdn_dt_bias, gdn_norm_w, gdn_w_out, router_w, router_b, moe_w_gate_up, moe_b_gate_up, moe_w_down,
           moe_b_down):
    batch, seq, d = x.shape
    t = batch * seq
    xs = x.reshape(t, d)
    cos_f, sin_s = _rope_tables(positions)
    mods = [[_adaln(c, ada_w[i, j], ada_b[i, j]) for j in range(2)] for i in range(DEPTH)]
    moe_w = (moe_w_gate_up.astype(BF16), moe_b_gate_up[:, :, None, :], moe_w_down.astype(BF16),
             moe_b_down[:, :, None, :])

    def router(i):
        wr = jnp.pad(router_w[i], ((0, 0), (0, LANES - N_EXPERTS)))
        br = jnp.concatenate([router_b[i], jnp.full((LANES - N_EXPERTS,), NEG_INF, F32)]).reshape(1, LANES)
        return wr, br

    for i in range(DEPTH):
        shift, scale, gate = mods[i][0]
        if i == 0:
            h = _modulate(xs, scale, shift, seq, BF16)
        if i % 2 == 0:
            qkv = _matmul(h, moba_w_qkv[i // 2].astype(BF16), out_dtype=F32, tm=1024, tn=1024, name="moba_qkv")
            o = _moba_attention(qkv, cos_f, sin_s, batch, seq)
            y = _matmul(o, moba_w_o[i // 2].astype(BF16), out_dtype=F32, tm=1024, tn=1024, name="moba_out")
        else:
            j = i // 2
            y = _gated_deltanet(h, gdn_w_in[j], gdn_conv_w[j], gdn_a_log[j], gdn_dt_bias[j], gdn_norm_w[j],
                                gdn_w_out[j], batch, seq)
        shift2, scale2, gate2 = mods[i][1]
        xs, h32, top_idx, gate_w = _post(xs, y, gate, ln_g[i, 0], ln_b[i, 0], seq, nxt=(scale2, shift2),
                                         router=router(i), h_tiled=True)
        nxt = None
        if i + 1 < DEPTH:
            shift_n, scale_n, _ = mods[i + 1][0]
            nxt = (scale_n, shift_n)
        res = _moe_sublayer(xs, h32, top_idx, gate_w, gate2, ln_g[i, 1], ln_b[i, 1], seq, moe_w, i, nxt)
        xs = res[0]
        if nxt is not None:
            h = res[1]
    return xs.reshape(batch, seq, d)
```

```python
import functools
import math

import jax
import jax.numpy as jnp
from jax import lax
from jax.experimental import pallas as pl
from jax.experimental.pallas import tpu as pltpu

F32 = jnp.float32
BF16 = jnp.bfloat16

D_MODEL = 2048
DEPTH = 2
DEEPNORM_ALPHA = float((2 * DEPTH) ** 0.25)
LN_EPS = 1e-5
MOBA_HEADS = 16
HEAD_DIM = 128
MOBA_BLOCK = 256
MOBA_TOPK = 3
ROPE_THETA = 500000.0
ROPE_DIM = HEAD_DIM // 4
NEG_INF = -1e30
GDN_K_HEADS = 16
GDN_V_HEADS = 32
GDN_KEY_DIM = GDN_K_HEADS * HEAD_DIM
GDN_VALUE_DIM = GDN_V_HEADS * HEAD_DIM
GDN_CONV_DIM = 2 * GDN_KEY_DIM + GDN_VALUE_DIM
GDN_MAIN_DIM = GDN_CONV_DIM + GDN_VALUE_DIM
GDN_CONV = 4
GDN_CHUNK = 128
GDN_NORM_EPS = 1e-6
L2_EPS = 1e-6
N_EXPERTS = 32
MOE_TOPK = 4
SWIGLU_ALPHA = 1.702
SWIGLU_LIMIT = 7.0

LANES = 128
VMEM_LIMIT = 50 * 1024 * 1024

MOE_TM = 512
MOE_TN = 512
GATHER_ROWS = 1024
POST_TM = 256


def _cparams(sem):
    return pltpu.CompilerParams(dimension_semantics=sem, vmem_limit_bytes=VMEM_LIMIT)


def _mm_kernel(*refs, silu_a, has_bias):
    if has_bias:
        a_ref, b_ref, bias_ref, o_ref = refs
    else:
        a_ref, b_ref, o_ref = refs
    a = a_ref[...]
    if silu_a:
        a = a.astype(F32)
        a = a * jax.nn.sigmoid(a)
    acc = jnp.dot(a.astype(BF16), b_ref[...], preferred_element_type=F32)
    if has_bias:
        acc = acc + bias_ref[...]
    o_ref[...] = acc.astype(o_ref.dtype)


def _matmul(a, b, *, out_dtype, tm, tn, bias=None, silu_a=False, name="mm"):
    m, k = a.shape
    n = b.shape[1]
    tm = min(tm, m)
    tn = min(tn, n)
    assert m % tm == 0 and n % tn == 0, (m, n, tm, tn)
    in_specs = [pl.BlockSpec((tm, k), lambda i, j: (i, 0)),
                pl.BlockSpec((k, tn), lambda i, j: (0, j))]
    args = [a, b]
    if bias is not None:
        in_specs.append(pl.BlockSpec((1, tn), lambda i, j: (0, j)))
        args.append(bias.reshape(1, n).astype(F32))
    return pl.pallas_call(
        functools.partial(_mm_kernel, silu_a=silu_a, has_bias=bias is not None),
        grid=(m // tm, n // tn),
        in_specs=in_specs,
        out_specs=pl.BlockSpec((tm, tn), lambda i, j: (i, j)),
        out_shape=jax.ShapeDtypeStruct((m, n), out_dtype),
        compiler_params=_cparams(("parallel", "parallel")),
        name=name,
    )(*args)


def _modulate_kernel(x_ref, scale_ref, shift_ref, h_ref):
    h = x_ref[...] * (1.0 + scale_ref[0]) + shift_ref[0]
    h_ref[...] = h.astype(h_ref.dtype)


def _modulate(x, scale, shift, seq, out_dtype):
    t, d = x.shape
    tm = POST_TM
    per_b = seq // tm
    vec = pl.BlockSpec((1, 1, d), lambda i: (i // per_b, 0, 0))
    return pl.pallas_call(
        _modulate_kernel,
        grid=(t // tm,),
        in_specs=[pl.BlockSpec((tm, d), lambda i: (i, 0)), vec, vec],
        out_specs=pl.BlockSpec((tm, d), lambda i: (i, 0)),
        out_shape=jax.ShapeDtypeStruct((t, d), out_dtype),
        compiler_params=_cparams(("parallel",)),
        name="modulate",
    )(x, scale[:, None, :], shift[:, None, :])


def _top4(logits):
    lane = lax.broadcasted_iota(jnp.int32, logits.shape, 1).astype(F32)
    cur = logits
    vals, idxs = [], []
    for _ in range(MOE_TOPK):
        m = jnp.max(cur, axis=1, keepdims=True)
        idx = jnp.min(jnp.where(cur == m, lane, float(LANES)), axis=1, keepdims=True)
        vals.append(m)
        idxs.append(idx)
        cur = jnp.where(lane == idx, -3.0e38, cur)
    exps = [jnp.exp(v - vals[0]) for v in vals]
    denom = exps[0] + exps[1] + exps[2] + exps[3]
    idx_out = jnp.zeros_like(logits)
    gate_out = jnp.zeros_like(logits)
    for k in range(MOE_TOPK):
        idx_out = jnp.where(lane == float(k), idxs[k], idx_out)
        gate_out = jnp.where(lane == float(k), exps[k] / denom, gate_out)
    return idx_out.astype(jnp.int32), gate_out


def _store_tiled(ref, val, group=1, member=0):
    n, d = val.shape
    s_per = d // LANES
    for s in range(s_per):
        ref[pl.ds(member * s_per + s, n, stride=group * s_per), :] = val[:, s * LANES:(s + 1) * LANES]


def _load_tiled(ref, member, group, d=D_MODEL):
    s_per = d // LANES
    n = ref.shape[0] // (group * s_per)
    return jnp.concatenate([ref[pl.ds(member * s_per + s, n, stride=group * s_per), :] for s in range(s_per)],
                           axis=1)


def _post_kernel(*refs, moe_in, has_next, has_router, h_tiled, d):
    it = iter(refs)
    if moe_in:
        cur_idx_ref = next(it)
        nxt_idx_ref = next(it)
    x_ref = next(it)
    y_ref = next(it)
    w4_ref = next(it) if moe_in else None
    gate_ref = next(it)
    lng_ref = next(it)
    lnb_ref = next(it)
    if has_next:
        scale_ref = next(it)
        shift_ref = next(it)
    if has_router:
        wr_ref = next(it)
        br_ref = next(it)
    xo_ref = next(it)
    if has_next:
        h_ref = next(it)
    if has_router:
        idx_ref = next(it)
        gw_ref = next(it)
    if moe_in:
        ybuf = next(it)
        sems = next(it)

    if moe_in:
        i = pl.program_id(0)
        n_rows = x_ref.shape[0] * MOE_TOPK
        s_per = d // LANES

        def start_rows(src_idx_ref, slot):
            def issue(r, carry):
                row = src_idx_ref[0, 0, r]
                pltpu.make_async_copy(y_ref.at[pl.ds(row * s_per, s_per), :],
                                      ybuf.at[slot, pl.ds(r * s_per, s_per), :], sems.at[slot]).start()
                return carry
            lax.fori_loop(0, n_rows, issue, 0, unroll=8)

        @pl.when(i == 0)
        def _():
            start_rows(cur_idx_ref, 0)

        @pl.when(i + 1 < pl.num_programs(0))
        def _():
            start_rows(nxt_idx_ref, (i + 1) % 2)

        slot = i % 2
        pltpu.make_async_copy(y_ref.at[pl.ds(0, n_rows * s_per), :], ybuf.at[slot], sems.at[slot]).wait()
        yb = ybuf.at[slot]
        w4 = w4_ref[...]
        y = w4[:, 0:1] * _load_tiled(yb, 0, MOE_TOPK)
        for k in range(1, MOE_TOPK):
            y = y + w4[:, k:k + 1] * _load_tiled(yb, k, MOE_TOPK)
    else:
        y = y_ref[...].astype(F32)
    z = DEEPNORM_ALPHA * x_ref[...] + (1.0 + gate_ref[0]) * y
    mu = jnp.mean(z, axis=1, keepdims=True)
    zc = z - mu
    var = jnp.mean(zc * zc, axis=1, keepdims=True)
    xn = zc * lax.rsqrt(var + LN_EPS) * lng_ref[...] + lnb_ref[...]
    xo_ref[...] = xn
    if has_next:
        h = xn * (1.0 + scale_ref[0]) + shift_ref[0]
        if h_tiled:
            _store_tiled(h_ref, _pack_bf16_pairs(h))
        else:
            h_ref[...] = h.astype(h_ref.dtype)
    if has_router:
        logits = jnp.dot(h, wr_ref[...], precision=lax.Precision.HIGHEST,
                         preferred_element_type=F32) + br_ref[...]
        idx, gw = _top4(logits)
        idx_ref[...] = idx
        gw_ref[...] = gw


def _post(x, y, gate, lng, lnb, seq, *, w4=None, nxt=None, router=None, h_tiled=False):
    t, d = x.shape
    tm = POST_TM
    per_b = seq // tm
    s_per = d // LANES
    row = pl.BlockSpec((tm, d), lambda i: (i, 0))
    vec = pl.BlockSpec((1, 1, d), lambda i: (i // per_b, 0, 0))
    full = lambda a: pl.BlockSpec(a.shape, lambda i: (0,) * a.ndim)
    moe_in = w4 is not None
    n_steps = t // tm
    scratch = []
    if moe_in:
        w4, dest = w4
        n_rows = tm * MOE_TOPK
        dest3 = dest.reshape(n_steps, 1, n_rows)
        smem = lambda f: pl.BlockSpec((1, 1, n_rows), f, memory_space=pltpu.SMEM)
        args = [dest3, dest3, x, y, w4]
        in_specs = [smem(lambda i: (i, 0, 0)), smem(lambda i: (jnp.minimum(i + 1, n_steps - 1), 0, 0)),
                    row, pl.BlockSpec(memory_space=pl.ANY), pl.BlockSpec((tm, LANES), lambda i: (i, 0))]
        scratch = [pltpu.VMEM((2, n_rows * s_per, LANES), F32), pltpu.SemaphoreType.DMA((2,))]
    else:
        args = [x, y]
        in_specs = [row, row]
    lng2, lnb2 = lng.reshape(1, d), lnb.reshape(1, d)
    args += [gate[:, None, :], lng2, lnb2]
    in_specs += [vec, full(lng2), full(lnb2)]
    out_shape = [jax.ShapeDtypeStruct((t, d), F32)]
    out_specs = [row]
    if nxt is not None:
        args += [nxt[0][:, None, :], nxt[1][:, None, :]]
        in_specs += [vec, vec]
        if h_tiled:
            out_shape.append(jax.ShapeDtypeStruct((t * s_per // 2, LANES), jnp.uint32))
            out_specs.append(pl.BlockSpec((tm * s_per // 2, LANES), lambda i: (i, 0)))
        else:
            out_shape.append(jax.ShapeDtypeStruct((t, d), BF16))
            out_specs.append(row)
    if router is not None:
        wr, br = router
        args += [wr, br]
        in_specs += [full(wr), full(br)]
        lane_row = pl.BlockSpec((tm, LANES), lambda i: (i, 0))
        out_shape += [jax.ShapeDtypeStruct((t, LANES), jnp.int32), jax.ShapeDtypeStruct((t, LANES), F32)]
        out_specs += [lane_row, lane_row]
    return pl.pallas_call(
        functools.partial(_post_kernel, moe_in=moe_in, has_next=nxt is not None,
                          has_router=router is not None, h_tiled=h_tiled, d=d),
        grid=(n_steps,),
        in_specs=in_specs,
        out_specs=out_specs,
        out_shape=out_shape,
        scratch_shapes=scratch,
        compiler_params=_cparams(("arbitrary",) if moe_in else ("parallel",)),
        name="moe_combine_post" if moe_in else "deepnorm_post",
    )(*args)


def _gather_kernel(idx_ref, src_ref, out_ref, sem, *, rows, s_per):
    def issue(r, carry):
        tok = idx_ref[0, 0, r]
        pltpu.make_async_copy(src_ref.at[pl.ds(tok * s_per, s_per), :],
                              out_ref.at[pl.ds(r * s_per, s_per), :], sem).start()
        return carry

    lax.fori_loop(0, rows, issue, 0, unroll=8)
    pltpu.make_async_copy(src_ref.at[pl.ds(0, rows * s_per), :], out_ref, sem).wait()


def _gather_rows(src, idx, d=D_MODEL, rows=GATHER_ROWS):
    n = idx.shape[0]
    rows = min(rows, n)
    assert n % rows == 0
    s_per = d // LANES
    return pl.pallas_call(
        functools.partial(_gather_kernel, rows=rows, s_per=s_per),
        grid=(n // rows,),
        in_specs=[pl.BlockSpec((1, 1, rows), lambda i: (i, 0, 0), memory_space=pltpu.SMEM),
                  pl.BlockSpec(memory_space=pl.ANY)],
        out_specs=pl.BlockSpec((rows * s_per, LANES), lambda i: (i, 0)),
        out_shape=jax.ShapeDtypeStruct((n * s_per, LANES), src.dtype),
        scratch_shapes=[pltpu.SemaphoreType.DMA],
        compiler_params=_cparams(("parallel",)),
        name="row_gather",
    )(idx.reshape(n // rows, 1, rows), src)


def _pack_bf16_pairs(h):
    half = h.shape[1] // 2
    bits = lax.bitcast_convert_type(h, jnp.uint32)
    rounded = bits + jnp.uint32(0x7FFF) + ((bits >> 16) & jnp.uint32(1))
    return (rounded[:, half:] & jnp.uint32(0xFFFF0000)) | (rounded[:, :half] >> 16)


def _unpack_bf16_pairs(words):
    lo = lax.bitcast_convert_type(words << 16, F32).astype(BF16)
    hi = lax.bitcast_convert_type(words & jnp.uint32(0xFFFF0000), F32).astype(BF16)
    return jnp.concatenate([lo, hi], axis=1)


def _expert_changed(te_ref, i):
    return (i == 0) | (te_ref[i] != te_ref[jnp.maximum(i - 1, 0)])


def _moe_gu_kernel(te_ref, nv_ref, x_ref, wg_ref, wu_ref, bg_ref, bu_ref, o_ref, wg_s, wu_s):
    i = pl.program_id(1)

    @pl.when(i < nv_ref[0])
    def _():
        @pl.when(_expert_changed(te_ref, i))
        def _():
            wg_s[...] = wg_ref[0, 0].astype(BF16)
            wu_s[...] = wu_ref[0, 0].astype(BF16)

        x = _unpack_bf16_pairs(_load_tiled(x_ref, 0, 1, d=x_ref.shape[0] // MOE_TM * LANES))
        g = jnp.dot(x, wg_s[...], preferred_element_type=F32) + bg_ref[0, 0]
        u = jnp.dot(x, wu_s[...], preferred_element_type=F32) + bu_ref[0, 0]
        g = jnp.minimum(g, SWIGLU_LIMIT)
        u = jnp.clip(u, -SWIGLU_LIMIT, SWIGLU_LIMIT)
        o_ref[...] = (g * jax.nn.sigmoid(SWIGLU_ALPHA * g) * (u + 1.0)).astype(o_ref.dtype)

    @pl.when(i >= nv_ref[0])
    def _():
        o_ref[...] = jnp.zeros_like(o_ref)


def _moe_down_kernel(te_ref, nv_ref, a_ref, wd_ref, bd_ref, o_ref, wd_s):
    i = pl.program_id(1)

    @pl.when(i < nv_ref[0])
    def _():
        @pl.when(_expert_changed(te_ref, i))
        def _():
            wd_s[...] = wd_ref[0, 0].astype(BF16)

        y = jnp.dot(a_ref[...], wd_s[...], preferred_element_type=F32) + bd_ref[0, 0]
        for s in range(o_ref.shape[1]):
            o_ref[:, s, :] = y[:, s * LANES:(s + 1) * LANES]

    @pl.when(i >= nv_ref[0])
    def _():
        o_ref[...] = jnp.zeros_like(o_ref)


def _moe_experts(x_sorted, tile_expert, n_valid, w_gu, b_gu, w_d, b_d, layer):
    f, d = w_d.shape[2], w_d.shape[3]
    s_per = d // LANES
    s_in = s_per // 2
    n_slots = x_sorted.shape[0] // s_in
    tm, tn = MOE_TM, MOE_TN
    n_tiles = n_slots // tm
    nj = f // tn
    act = pl.pallas_call(
        _moe_gu_kernel,
        grid_spec=pltpu.PrefetchScalarGridSpec(
            num_scalar_prefetch=2,
            grid=(nj, n_tiles),
            in_specs=[pl.BlockSpec((tm * s_in, LANES), lambda j, i, te, nv: (i, 0)),
                      pl.BlockSpec((1, 1, d, tn), lambda j, i, te, nv: (layer, te[i], 0, j)),
                      pl.BlockSpec((1, 1, d, tn), lambda j, i, te, nv: (layer, te[i], 0, nj + j)),
                      pl.BlockSpec((1, 1, 1, tn), lambda j, i, te, nv: (layer, te[i], 0, j)),
                      pl.BlockSpec((1, 1, 1, tn), lambda j, i, te, nv: (layer, te[i], 0, nj + j))],
            out_specs=pl.BlockSpec((tm, tn), lambda j, i, te, nv: (i, j)),
            scratch_shapes=[pltpu.VMEM((d, tn), BF16), pltpu.VMEM((d, tn), BF16)],
        ),
        out_shape=jax.ShapeDtypeStruct((n_slots, f), BF16),
        compiler_params=_cparams(("arbitrary", "arbitrary")),
        name="moe_gate_up",
    )(tile_expert, n_valid, x_sorted, w_gu, w_gu, b_gu, b_gu)
    nh = 2
    dh = d // nh
    y = pl.pallas_call(
        _moe_down_kernel,
        grid_spec=pltpu.PrefetchScalarGridSpec(
            num_scalar_prefetch=2,
            grid=(nh, n_tiles),
            in_specs=[pl.BlockSpec((tm, f), lambda j, i, te, nv: (i, 0)),
                      pl.BlockSpec((1, 1, f, dh), lambda j, i, te, nv: (layer, te[i], 0, j)),
                      pl.BlockSpec((1, 1, 1, dh), lambda j, i, te, nv: (layer, te[i], 0, j))],
            out_specs=pl.BlockSpec((tm, s_per // nh, LANES), lambda j, i, te, nv: (i, j, 0)),
            scratch_shapes=[pltpu.VMEM((f, dh), BF16)],
        ),
        out_shape=jax.ShapeDtypeStruct((n_slots, s_per, LANES), F32),
        compiler_params=_cparams(("arbitrary", "arbitrary")),
        name="moe_down",
    )(tile_expert, n_valid, act, w_d, b_d)
    return y.reshape(n_slots * s_per, LANES)


def _moe_plan(top_idx):
    t = top_idx.shape[0]
    n_items = t * MOE_TOPK
    n_slots = n_items + N_EXPERTS * MOE_TM
    items = top_idx.reshape(n_items)
    onehot = (items[:, None] == jnp.arange(N_EXPERTS, dtype=jnp.int32)[None, :]).astype(jnp.int32)
    csum = jnp.cumsum(onehot, axis=0)
    counts = csum[-1]
    rank = jnp.sum(csum * onehot, axis=1) - 1
    padded = (counts + MOE_TM - 1) // MOE_TM * MOE_TM
    pad_end = jnp.cumsum(padded)
    pad_start = pad_end - padded
    dest = (pad_start[items] + rank).astype(jnp.int32)
    slot_tok = (jnp.arange(n_slots, dtype=jnp.int32) % t).at[dest].set(
        jnp.arange(n_items, dtype=jnp.int32) // MOE_TOPK)
    tile_start = jnp.arange(n_slots // MOE_TM, dtype=jnp.int32) * MOE_TM
    tile_expert = jnp.minimum(jnp.searchsorted(pad_end, tile_start, side='right'), N_EXPERTS - 1).astype(jnp.int32)
    n_valid = (pad_end[-1] // MOE_TM).astype(jnp.int32).reshape(1)
    return dest, slot_tok, tile_expert, n_valid


def _rope(x, cos_f, sin_s):
    lane = lax.broadcasted_iota(jnp.int32, x.shape, 1)
    half = ROPE_DIM // 2
    partner = jnp.where(lane < half, pltpu.roll(x, LANES - half, 1), pltpu.roll(x, half, 1))
    return x * cos_f + partner * sin_s


def _moba_kernel(q_ref, k_ref, v_ref, cos_ref, sin_ref, o_ref, *, n_blk):
    blk = MOBA_BLOCK
    cos_f, sin_s = cos_ref[...], sin_ref[...]
    kr = _rope(k_ref[...], cos_f, sin_s)
    means = [jnp.mean(kr[j * blk:(j + 1) * blk], axis=0, keepdims=True) for j in range(n_blk)]
    n_rows = max(n_blk, 8)
    if n_rows > n_blk:
        means.append(jnp.zeros((n_rows - n_blk, HEAD_DIM), F32))
    kmean = jnp.concatenate(means, axis=0).astype(BF16)
    ks = (kr * (HEAD_DIM ** -0.5)).astype(BF16)
    vt = v_ref[...].T.astype(BF16)
    qt = _rope(q_ref[...], cos_f, sin_s).T.astype(BF16)

    gate = jnp.dot(kmean, qt, preferred_element_type=F32)
    row_i = lax.broadcasted_iota(jnp.int32, gate.shape, 0)
    rowf = row_i.astype(F32)
    past = row_i < lax.broadcasted_iota(jnp.int32, gate.shape, 1) // blk
    cur = jnp.where(past, gate, NEG_INF)
    sel = jnp.zeros_like(gate)
    for _ in range(min(MOBA_TOPK, n_blk - 1)):
        m = jnp.max(cur, axis=0, keepdims=True)
        idx = jnp.min(jnp.where(cur == m, rowf, float(LANES)), axis=0, keepdims=True)
        hit = rowf == idx
        sel = jnp.where(hit & past, 1.0, sel)
        cur = jnp.where(hit, -3.0e38, cur)

    r_i = lax.broadcasted_iota(jnp.int32, (blk, blk), 0)
    c_i = lax.broadcasted_iota(jnp.int32, (blk, blk), 1)
    outs = []
    for qb in range(n_blk):
        cols = slice(qb * blk, (qb + 1) * blk)
        nk = (qb + 1) * blk
        s = jnp.dot(ks[:nk], qt[:, cols], preferred_element_type=F32)
        parts = [jnp.where(sel[j:j + 1, cols] > 0.0, s[j * blk:(j + 1) * blk], NEG_INF) for j in range(qb)]
        parts.append(jnp.where(r_i <= c_i, s[qb * blk:], NEG_INF))
        s = jnp.concatenate(parts, axis=0)
        p = jnp.exp(s - jnp.max(s, axis=0, keepdims=True))
        l = jnp.sum(p, axis=0, keepdims=True)
        acc = jnp.dot(vt[:, :nk], p.astype(BF16), preferred_element_type=F32)
        outs.append(acc / l)
    o_ref[...] = jnp.concatenate(outs, axis=1).T.astype(o_ref.dtype)


def _moba_attention(qkv, cos_f, sin_s, batch, seq):
    t = qkv.shape[0]
    n_blk = seq // MOBA_BLOCK
    h = MOBA_HEADS
    head = lambda off: pl.BlockSpec((seq, HEAD_DIM), lambda b, hh: (b, off + hh))
    table = pl.BlockSpec((seq, HEAD_DIM), lambda b, hh: (b, 0))
    return pl.pallas_call(
        functools.partial(_moba_kernel, n_blk=n_blk),
        grid=(batch, h),
        in_specs=[head(0), head(h), head(2 * h), table, table],
        out_specs=head(0),
        out_shape=jax.ShapeDtypeStruct((t, h * HEAD_DIM), BF16),
        compiler_params=_cparams(("parallel", "parallel")),
        name="moba_attention",
    )(qkv, qkv, qkv, cos_f, sin_s)


def _rope_tables(positions):
    half = ROPE_DIM // 2
    inv_freq = ROPE_THETA ** (-jnp.arange(0, ROPE_DIM, 2, dtype=F32) / ROPE_DIM)
    ang = positions.astype(F32).reshape(-1, 1) * inv_freq
    cos, sin = jnp.cos(ang), jnp.sin(ang)
    t = ang.shape[0]
    rest = HEAD_DIM - ROPE_DIM
    cos_f = jnp.concatenate([cos, cos, jnp.ones((t, rest), F32)], axis=1)
    sin_s = jnp.concatenate([-sin, sin, jnp.zeros((t, rest), F32)], axis=1)
    return cos_f, sin_s


def _shift_rows(x, k):
    rolled = pltpu.roll(x, k, 0)
    row = lax.broadcasted_iota(jnp.int32, (8, x.shape[1]), 0)
    return jnp.concatenate([jnp.where(row >= k, rolled[0:8], 0.0), rolled[8:]], axis=0)


def _conv_silu(x, w):
    y = x * w[GDN_CONV - 1:GDN_CONV, :]
    for j in range(GDN_CONV - 1):
        y = y + _shift_rows(x, GDN_CONV - 1 - j) * w[j:j + 1, :]
    return y * jax.nn.sigmoid(y)


def _l2_normalize(y):
    return y * lax.rsqrt(jnp.sum(y * y, axis=1, keepdims=True) + L2_EPS)


def _gdn_gates_kernel(ba_ref, alog_ref, dtb_ref, gb_ref, gbt_ref):
    x = ba_ref[...]
    lane = lax.broadcasted_iota(jnp.int32, x.shape, 1)
    row = lax.broadcasted_iota(jnp.int32, x.shape, 0)
    xa = x + dtb_ref[...]
    softplus = jnp.maximum(xa, 0.0) + jnp.log(1.0 + jnp.exp(-jnp.abs(xa)))
    g = -jnp.exp(alog_ref[...]) * softplus
    pos = row % GDN_CHUNK
    sh = 1
    while sh < GDN_CHUNK:
        g = g + jnp.where(pos >= sh, pltpu.roll(g, sh, 0), 0.0)
        sh *= 2
    out = jnp.where(lane < GDN_V_HEADS, jax.nn.sigmoid(x), g)
    gb_ref[...] = out
    gbt_ref[...] = out.T


def _gdn_gates(ba, a_log, dt_bias, batch, seq):
    pad = jnp.zeros((LANES - 2 * GDN_V_HEADS,), F32)
    zeros = jnp.zeros((GDN_V_HEADS,), F32)
    alog_row = jnp.concatenate([zeros, a_log.astype(F32), pad]).reshape(1, LANES)
    dtb_row = jnp.concatenate([zeros, dt_bias.astype(F32), pad]).reshape(1, LANES)
    return pl.pallas_call(
        _gdn_gates_kernel,
        grid=(batch,),
        in_specs=[pl.BlockSpec((seq, LANES), lambda b: (b, 0)),
                  pl.BlockSpec((1, LANES), lambda b: (0, 0)),
                  pl.BlockSpec((1, LANES), lambda b: (0, 0))],
        out_specs=[pl.BlockSpec((seq, LANES), lambda b: (b, 0)),
                   pl.BlockSpec((LANES, seq), lambda b: (b, 0))],
        out_shape=[jax.ShapeDtypeStruct((batch * seq, LANES), F32),
                   jax.ShapeDtypeStruct((batch * LANES, seq), F32)],
        compiler_params=_cparams(("parallel",)),
        name="gdn_gates",
    )(ba, alog_row, dtb_row)


def _bmm(a, b):
    return jnp.einsum('nij,njk->nik', a.astype(BF16), b.astype(BF16), preferred_element_type=F32)


def _gdn_core_kernel(q_ref, k_ref, v_ref, z_ref, cwq_ref, cwk_ref, cwv_ref, gb_ref, gbt_ref, nw_ref, o_ref,
                     b_s, c_s, qe_s, o0_s, eg_s, st_s, *, n_chunks):
    c = GDN_CHUNK
    hk = pl.program_id(1)
    q2 = _l2_normalize(_conv_silu(q_ref[...], cwq_ref[...]))
    k2 = _l2_normalize(_conv_silu(k_ref[...], cwk_ref[...]))
    v2 = _conv_silu(v_ref[...], cwv_ref[...])
    q3 = (q2 * (HEAD_DIM ** -0.5)).reshape(n_chunks, c, HEAD_DIM)
    k3 = k2.reshape(n_chunks, c, HEAD_DIM)
    kt = k2.T
    qh, kh = q3.astype(BF16), k3.astype(BF16)
    kk = jnp.einsum('nid,njd->nij', kh, kh, preferred_element_type=F32)
    qk = jnp.einsum('nid,njd->nij', qh, kh, preferred_element_type=F32)
    r_i = lax.broadcasted_iota(jnp.int32, (c, c), 0)
    c_i = lax.broadcasted_iota(jnp.int32, (c, c), 1)
    tril = c_i <= r_i
    strict = c_i < r_i
    eye = (c_i == r_i).astype(F32)
    lane = lax.broadcasted_iota(jnp.int32, gb_ref.shape, 1)
    gb = gb_ref[...]

    for vh in range(2):
        hv = 2 * hk + vh
        beta = jnp.sum(jnp.where(lane == hv, gb, 0.0), axis=1, keepdims=True).reshape(n_chunks, c, 1)
        gcol = jnp.sum(jnp.where(lane == GDN_V_HEADS + hv, gb, 0.0), axis=1, keepdims=True).reshape(n_chunks, c, 1)
        grow = gbt_ref[pl.ds(GDN_V_HEADS + hv, 1), :]
        decay, kdt = [], []
        for n in range(n_chunks):
            gr = grow[:, n * c:(n + 1) * c]
            diff = gcol[n] - gr
            decay.append(jnp.exp(jnp.where(tril, diff, NEG_INF)))
            kdt.append(kt[:, n * c:(n + 1) * c] * jnp.exp(gr[:, c - 1:c] - gr))
        decay = jnp.stack(decay, axis=0)
        kdt = jnp.stack(kdt, axis=0)
        m = jnp.where(strict, beta * kk * decay, 0.0)
        attn = jnp.where(tril, qk * decay, 0.0)
        t = eye - jnp.where((r_i // 2 == c_i // 2), m, 0.0)
        mb = m.astype(BF16)
        s = 2
        while s < c:
            off = (r_i // (2 * s) == c_i // (2 * s)) & ((r_i % (2 * s)) >= s) & ((c_i % (2 * s)) < s)
            moff = jnp.where(off, mb, jnp.zeros_like(mb))
            tb = t.astype(BF16)
            t = t - _bmm(_bmm(tb, moff), tb)
            s *= 2
        eg = jnp.exp(gcol)
        v3 = v2[:, vh * HEAD_DIM:(vh + 1) * HEAD_DIM].reshape(n_chunks, c, HEAD_DIM)
        uw = _bmm(t, jnp.concatenate([v3 * beta, k3 * (beta * eg)], axis=-1))
        bc = _bmm(kdt, uw)
        ao = _bmm(attn, uw)
        b_s[vh] = bc[:, :, :HEAD_DIM]
        c_s[vh] = bc[:, :, HEAD_DIM:].astype(BF16)
        o0_s[vh] = ao[:, :, :HEAD_DIM]
        qe_s[vh] = (q3 * eg - ao[:, :, HEAD_DIM:]).astype(BF16)
        glast = gcol[:, c - 1:c, :]
        eg_s[vh] = jnp.broadcast_to(jnp.exp(glast), (n_chunks, 8, HEAD_DIM))
        st_s[vh] = jnp.zeros((HEAD_DIM, HEAD_DIM), F32)

    nw = nw_ref[...]

    def chunk_step(n, carry):
        rows = pl.ds(pl.multiple_of(n * c, c), c)
        for vh in range(2):
            state = st_s[vh]
            sb = state.astype(BF16)
            o = jnp.dot(qe_s[vh, n], sb, preferred_element_type=F32) + o0_s[vh, n]
            st_s[vh] = (state * eg_s[vh, n][0:1, :] + b_s[vh, n]
                        - jnp.dot(c_s[vh, n], sb, preferred_element_type=F32))
            zz = z_ref[rows, vh * HEAD_DIM:(vh + 1) * HEAD_DIM]
            on = o * lax.rsqrt(jnp.mean(o * o, axis=1, keepdims=True) + GDN_NORM_EPS)
            o_ref[rows, vh * HEAD_DIM:(vh + 1) * HEAD_DIM] = (on * nw * (zz * jax.nn.sigmoid(zz))).astype(o_ref.dtype)
        return carry

    lax.fori_loop(0, n_chunks, chunk_step, 0)


def _gdn_core(proj, conv_w, gb, gbt, norm_w, batch, seq):
    n_chunks = seq // GDN_CHUNK
    hd = HEAD_DIM
    vblk0 = 2 * GDN_KEY_DIM // (2 * hd)
    zblk0 = GDN_CONV_DIM // (2 * hd)
    big = lambda dt: pltpu.VMEM((2, n_chunks, GDN_CHUNK, hd), dt)
    return pl.pallas_call(
        functools.partial(_gdn_core_kernel, n_chunks=n_chunks),
        grid=(batch, GDN_K_HEADS),
        in_specs=[pl.BlockSpec((seq, hd), lambda b, h: (b, h)),
                  pl.BlockSpec((seq, hd), lambda b, h: (b, GDN_K_HEADS + h)),
                  pl.BlockSpec((seq, 2 * hd), lambda b, h: (b, vblk0 + h)),
                  pl.BlockSpec((seq, 2 * hd), lambda b, h: (b, zblk0 + h)),
                  pl.BlockSpec((GDN_CONV, hd), lambda b, h: (0, h)),
                  pl.BlockSpec((GDN_CONV, hd), lambda b, h: (0, GDN_K_HEADS + h)),
                  pl.BlockSpec((GDN_CONV, 2 * hd), lambda b, h: (0, vblk0 + h)),
                  pl.BlockSpec((seq, LANES), lambda b, h: (b, 0)),
                  pl.BlockSpec((LANES, seq), lambda b, h: (b, 0)),
                  pl.BlockSpec((1, hd), lambda b, h: (0, 0))],
        out_specs=pl.BlockSpec((seq, 2 * hd), lambda b, h: (b, h)),
        out_shape=jax.ShapeDtypeStruct((batch * seq, GDN_VALUE_DIM), BF16),
        scratch_shapes=[big(F32), big(BF16), big(BF16), big(F32),
                        pltpu.VMEM((2, n_chunks, 8, hd), F32), pltpu.VMEM((2, hd, hd), F32)],
        compiler_params=_cparams(("parallel", "arbitrary")),
        name="gdn_core",
    )(proj, proj, proj, proj, conv_w, conv_w, conv_w, gb, gbt, norm_w.reshape(1, hd).astype(F32))


def _gated_deltanet(h, w_in, conv_w, a_log, dt_bias, norm_w, w_out, batch, seq):
    w_main = w_in[:, :GDN_MAIN_DIM].astype(BF16)
    w_ba = jnp.pad(w_in[:, GDN_MAIN_DIM:], ((0, 0), (0, LANES - 2 * GDN_V_HEADS))).astype(BF16)
    proj = _matmul(h, w_main, out_dtype=F32, tm=1024, tn=1024, name="gdn_in_proj")
    ba = _matmul(h, w_ba, out_dtype=F32, tm=1024, tn=LANES, name="gdn_ba_proj")
    gb, gbt = _gdn_gates(ba, a_log, dt_bias, batch, seq)
    o = _gdn_core(proj, conv_w, gb, gbt, norm_w, batch, seq)
    return _matmul(o, w_out.astype(BF16), out_dtype=F32, tm=512, tn=1024, name="gdn_out_proj")


def _adaln(c, w, b):
    m = _matmul(c, w.astype(BF16), out_dtype=F32, tm=c.shape[0], tn=1024, bias=b, silu_a=True, name="adaln")
    d = w.shape[0]
    return m[:, :d], m[:, d:2 * d], m[:, 2 * d:]


def _moe_sublayer(x, h, top_idx, gate_w, gate_c, lng, lnb, seq, moe_w, layer, nxt):
    dest, slot_tok, tile_expert, n_valid = _moe_plan(top_idx[:, :MOE_TOPK])
    x_sorted = _gather_rows(h, slot_tok, d=D_MODEL // 2)
    y_sorted = _moe_experts(x_sorted, tile_expert, n_valid, *moe_w, layer)
    return _post(x, y_sorted, gate_c, lng, lnb, seq, w4=(gate_w, dest), nxt=nxt)


def kernel(x, c, positions, ada_w, ada_b, ln_g, ln_b, moba_w_qkv, moba_w_o, gdn_w_in, gdn_conv_w, gdn_a_log,
           gdn_dt_bias, gdn_norm_w, gdn_w_out, router_w, router_b, moe_w_gate_up, moe_b_gate_up, moe_w_down,
           moe_b_down):
    batch, seq, d = x.shape
    t = batch * seq
    xs = x.reshape(t, d)
    cos_f, sin_s = _rope_tables(positions)
    mods = [[_adaln(c, ada_w[i, j], ada_b[i, j]) for j in range(2)] for i in range(DEPTH)]
    moe_w = (moe_w_gate_up, moe_b_gate_up[:, :, None, :], moe_w_down, moe_b_down[:, :, None, :])

    def router(i):
        wr = jnp.pad(router_w[i], ((0, 0), (0, LANES - N_EXPERTS)))
        br = jnp.concatenate([router_b[i], jnp.full((LANES - N_EXPERTS,), NEG_INF, F32)]).reshape(1, LANES)
        return wr, br

    for i in range(DEPTH):
        shift, scale, gate = mods[i][0]
        if i == 0:
            h = _modulate(xs, scale, shift, seq, BF16)
        if i % 2 == 0:
            qkv = _matmul(h, moba_w_qkv[i // 2].astype(BF16), out_dtype=F32, tm=1024, tn=1024, name="moba_qkv")
            o = _moba_attention(qkv, cos_f, sin_s, batch, seq)
            y = _matmul(o, moba_w_o[i // 2].astype(BF16), out_dtype=F32, tm=1024, tn=1024, name="moba_out")
        else:
            j = i // 2
            y = _gated_deltanet(h, gdn_w_in[j], gdn_conv_w[j], gdn_a_log[j], gdn_dt_bias[j], gdn_norm_w[j],
                                gdn_w_out[j], batch, seq)
        shift2, scale2, gate2 = mods[i][1]
        xs, h32, top_idx, gate_w = _post(xs, y, gate, ln_g[i, 0], ln_b[i, 0], seq, nxt=(scale2, shift2),
                                         router=router(i), h_tiled=True)
        nxt = None
        if i + 1 < DEPTH:
            shift_n, scale_n, _ = mods[i + 1][0]
            nxt = (scale_n, shift_n)
        res = _moe_sublayer(xs, h32, top_idx, gate_w, gate2, ln_g[i, 1], ln_b[i, 1], seq, moe_w, i, nxt)
        xs = res[0]
        if nxt is not None:
            h = res[1]
    return xs.reshape(batch, seq, d)
```

```python
import functools
import math

import jax
import jax.numpy as jnp
from jax import lax
from jax.experimental import pallas as pl
from jax.experimental.pallas import tpu as pltpu

F32 = jnp.float32
BF16 = jnp.bfloat16

D_MODEL = 2048
DEPTH = 2
DEEPNORM_ALPHA = float((2 * DEPTH) ** 0.25)
LN_EPS = 1e-5
MOBA_HEADS = 16
HEAD_DIM = 128
MOBA_BLOCK = 256
MOBA_TOPK = 3
MOBA_HEADS_PER_STEP = 2
ROPE_THETA = 500000.0
ROPE_DIM = HEAD_DIM // 4
NEG_INF = -1e30
GDN_K_HEADS = 16
GDN_V_HEADS = 32
GDN_KEY_DIM = GDN_K_HEADS * HEAD_DIM
GDN_VALUE_DIM = GDN_V_HEADS * HEAD_DIM
GDN_CONV_DIM = 2 * GDN_KEY_DIM + GDN_VALUE_DIM
GDN_MAIN_DIM = GDN_CONV_DIM + GDN_VALUE_DIM
GDN_CONV = 4
GDN_CHUNK = 128
GDN_NORM_EPS = 1e-6
L2_EPS = 1e-6
N_EXPERTS = 32
MOE_TOPK = 4
SWIGLU_ALPHA = 1.702
SWIGLU_LIMIT = 7.0

LANES = 128
VMEM_LIMIT = 50 * 1024 * 1024
DMA_PRIORITIES = 2
COMBINE_PITCH = 20

MOE_TM = 512
MOE_TN = 1024
GATHER_ROWS = 1024
POST_TM = 256


def _cparams(sem):
    return pltpu.CompilerParams(dimension_semantics=sem, vmem_limit_bytes=VMEM_LIMIT)


def _mm_kernel(*refs, silu_a, has_bias):
    if has_bias:
        a_ref, b_ref, bias_ref, o_ref = refs
    else:
        a_ref, b_ref, o_ref = refs
    a = a_ref[...]
    if silu_a:
        a = a.astype(F32)
        a = a * jax.nn.sigmoid(a)
    acc = jnp.dot(a.astype(BF16), b_ref[...], preferred_element_type=F32)
    if has_bias:
        acc = acc + bias_ref[...]
    o_ref[...] = acc.astype(o_ref.dtype)


def _matmul(a, b, *, out_dtype, tm, tn, bias=None, silu_a=False, name="mm"):
    m, k = a.shape
    n = b.shape[1]
    tm = min(tm, m)
    tn = min(tn, n)
    assert m % tm == 0 and n % tn == 0, (m, n, tm, tn)
    in_specs = [pl.BlockSpec((tm, k), lambda i, j: (i, 0)),
                pl.BlockSpec((k, tn), lambda i, j: (0, j))]
    args = [a, b]
    if bias is not None:
        in_specs.append(pl.BlockSpec((1, tn), lambda i, j: (0, j)))
        args.append(bias.reshape(1, n).astype(F32))
    return pl.pallas_call(
        functools.partial(_mm_kernel, silu_a=silu_a, has_bias=bias is not None),
        grid=(m // tm, n // tn),
        in_specs=in_specs,
        out_specs=pl.BlockSpec((tm, tn), lambda i, j: (i, j)),
        out_shape=jax.ShapeDtypeStruct((m, n), out_dtype),
        compiler_params=_cparams(("parallel", "parallel")),
        name=name,
    )(*args)


def _modulate_kernel(x_ref, scale_ref, shift_ref, h_ref):
    h = x_ref[...] * (1.0 + scale_ref[0]) + shift_ref[0]
    h_ref[...] = h.astype(h_ref.dtype)


def _modulate(x, scale, shift, seq, out_dtype):
    t, d = x.shape
    tm = POST_TM
    per_b = seq // tm
    vec = pl.BlockSpec((1, 1, d), lambda i: (i // per_b, 0, 0))
    return pl.pallas_call(
        _modulate_kernel,
        grid=(t // tm,),
        in_specs=[pl.BlockSpec((tm, d), lambda i: (i, 0)), vec, vec],
        out_specs=pl.BlockSpec((tm, d), lambda i: (i, 0)),
        out_shape=jax.ShapeDtypeStruct((t, d), out_dtype),
        compiler_params=_cparams(("parallel",)),
        name="modulate",
    )(x, scale[:, None, :], shift[:, None, :])


def _top4(logits):
    lane = lax.broadcasted_iota(jnp.int32, logits.shape, 1).astype(F32)
    cur = logits
    vals, idxs = [], []
    for _ in range(MOE_TOPK):
        m = jnp.max(cur, axis=1, keepdims=True)
        idx = jnp.min(jnp.where(cur == m, lane, float(LANES)), axis=1, keepdims=True)
        vals.append(m)
        idxs.append(idx)
        cur = jnp.where(lane == idx, -3.0e38, cur)
    exps = [jnp.exp(v - vals[0]) for v in vals]
    denom = exps[0] + exps[1] + exps[2] + exps[3]
    idx_out = jnp.zeros_like(logits)
    gate_out = jnp.zeros_like(logits)
    for k in range(MOE_TOPK):
        idx_out = jnp.where(lane == float(k), idxs[k], idx_out)
        gate_out = jnp.where(lane == float(k), exps[k] / denom, gate_out)
    return idx_out.astype(jnp.int32), gate_out


def _store_tiled(ref, val, group=1, member=0):
    n, d = val.shape
    s_per = d // LANES
    for s in range(s_per):
        ref[pl.ds(member * s_per + s, n, stride=group * s_per), :] = val[:, s * LANES:(s + 1) * LANES]


def _load_tiled(ref, member, group, d=D_MODEL):
    s_per = d // LANES
    n = ref.shape[0] // (group * s_per)
    return jnp.concatenate([ref[pl.ds(member * s_per + s, n, stride=group * s_per), :] for s in range(s_per)],
                           axis=1)


def _post_kernel(*refs, moe_in, has_next, has_router, h_tiled, d):
    it = iter(refs)
    if moe_in:
        cur_idx_ref = next(it)
        nxt_idx_ref = next(it)
    x_ref = next(it)
    y_ref = next(it)
    w4_ref = next(it) if moe_in else None
    gate_ref = next(it)
    lng_ref = next(it)
    lnb_ref = next(it)
    if has_next:
        scale_ref = next(it)
        shift_ref = next(it)
    if has_router:
        wr_ref = next(it)
        br_ref = next(it)
    xo_ref = next(it)
    if has_next:
        h_ref = next(it)
    if has_router:
        idx_ref = next(it)
        gw_ref = next(it)
    if moe_in:
        ybuf = next(it)
        sems = next(it)

    if moe_in:
        i = pl.program_id(0)
        tm = x_ref.shape[0]
        n_rows = tm * MOE_TOPK
        s_per = d // LANES

        def start_rows(src_idx_ref, slot):
            def issue(tok, carry):
                for k in range(MOE_TOPK):
                    row = src_idx_ref[0, 0, tok * MOE_TOPK + k]
                    pltpu.make_async_copy(y_ref.at[pl.ds(row * s_per, s_per), :],
                                          ybuf.at[slot, pl.ds((k * tm + tok) * COMBINE_PITCH, s_per), :],
                                          sems.at[slot]).start(priority=k % DMA_PRIORITIES)
                return carry
            lax.fori_loop(0, tm, issue, 0, unroll=2)

        @pl.when(i == 0)
        def _():
            start_rows(cur_idx_ref, 0)

        @pl.when(i + 1 < pl.num_programs(0))
        def _():
            start_rows(nxt_idx_ref, (i + 1) % 2)

        slot = i % 2
        pltpu.make_async_copy(y_ref.at[pl.ds(0, n_rows * s_per), :],
                              ybuf.at[slot, pl.ds(0, n_rows * s_per), :], sems.at[slot]).wait()
        yb = ybuf.at[slot]

        def expert_rows(k):
            return jnp.concatenate([yb[pl.ds(k * tm * COMBINE_PITCH + s, tm, stride=COMBINE_PITCH), :]
                                    for s in range(s_per)], axis=1)

        w4 = w4_ref[...]
        y = w4[:, 0:1] * expert_rows(0)
        for k in range(1, MOE_TOPK):
            y = y + w4[:, k:k + 1] * expert_rows(k)
    else:
        y = y_ref[...].astype(F32)
    z = DEEPNORM_ALPHA * x_ref[...] + (1.0 + gate_ref[0]) * y
    mu = jnp.mean(z, axis=1, keepdims=True)
    zc = z - mu
    var = jnp.mean(zc * zc, axis=1, keepdims=True)
    xn = zc * lax.rsqrt(var + LN_EPS) * lng_ref[...] + lnb_ref[...]
    xo_ref[...] = xn
    if has_next:
        h = xn * (1.0 + scale_ref[0]) + shift_ref[0]
        if h_tiled:
            _store_tiled(h_ref, _pack_bf16_pairs(h))
        else:
            h_ref[...] = h.astype(h_ref.dtype)
    if has_router:
        logits = jnp.dot(h, wr_ref[...], precision=lax.Precision.HIGHEST,
                         preferred_element_type=F32) + br_ref[...]
        idx, gw = _top4(logits)
        idx_ref[...] = idx
        gw_ref[...] = gw


def _post(x, y, gate, lng, lnb, seq, *, w4=None, nxt=None, router=None, h_tiled=False):
    t, d = x.shape
    tm = POST_TM
    per_b = seq // tm
    s_per = d // LANES
    row = pl.BlockSpec((tm, d), lambda i: (i, 0))
    vec = pl.BlockSpec((1, 1, d), lambda i: (i // per_b, 0, 0))
    full = lambda a: pl.BlockSpec(a.shape, lambda i: (0,) * a.ndim)
    moe_in = w4 is not None
    n_steps = t // tm
    scratch = []
    if moe_in:
        w4, dest = w4
        n_rows = tm * MOE_TOPK
        dest3 = dest.reshape(n_steps, 1, n_rows)
        smem = lambda f: pl.BlockSpec((1, 1, n_rows), f, memory_space=pltpu.SMEM)
        args = [dest3, dest3, x, y, w4]
        in_specs = [smem(lambda i: (i, 0, 0)), smem(lambda i: (jnp.minimum(i + 1, n_steps - 1), 0, 0)),
                    row, pl.BlockSpec(memory_space=pl.ANY), pl.BlockSpec((tm, LANES), lambda i: (i, 0))]
        scratch = [pltpu.VMEM((2, n_rows * COMBINE_PITCH, LANES), F32), pltpu.SemaphoreType.DMA((2,))]
    else:
        args = [x, y]
        in_specs = [row, row]
    lng2, lnb2 = lng.reshape(1, d), lnb.reshape(1, d)
    args += [gate[:, None, :], lng2, lnb2]
    in_specs += [vec, full(lng2), full(lnb2)]
    out_shape = [jax.ShapeDtypeStruct((t, d), F32)]
    out_specs = [row]
    if nxt is not None:
        args += [nxt[0][:, None, :], nxt[1][:, None, :]]
        in_specs += [vec, vec]
        if h_tiled:
            out_shape.append(jax.ShapeDtypeStruct((t * s_per // 2, LANES), jnp.uint32))
            out_specs.append(pl.BlockSpec((tm * s_per // 2, LANES), lambda i: (i, 0)))
        else:
            out_shape.append(jax.ShapeDtypeStruct((t, d), BF16))
            out_specs.append(row)
    if router is not None:
        wr, br = router
        args += [wr, br]
        in_specs += [full(wr), full(br)]
        lane_row = pl.BlockSpec((tm, LANES), lambda i: (i, 0))
        out_shape += [jax.ShapeDtypeStruct((t, LANES), jnp.int32), jax.ShapeDtypeStruct((t, LANES), F32)]
        out_specs += [lane_row, lane_row]
    return pl.pallas_call(
        functools.partial(_post_kernel, moe_in=moe_in, has_next=nxt is not None,
                          has_router=router is not None, h_tiled=h_tiled, d=d),
        grid=(n_steps,),
        in_specs=in_specs,
        out_specs=out_specs,
        out_shape=out_shape,
        scratch_shapes=scratch,
        compiler_params=_cparams(("arbitrary",) if moe_in else ("parallel",)),
        name="moe_combine_post" if moe_in else "deepnorm_post",
    )(*args)


def _gather_kernel(idx_ref, src_ref, out_ref, sem, *, rows, s_per):
    def issue(pair, carry):
        for prio in range(DMA_PRIORITIES):
            r = pair * DMA_PRIORITIES + prio
            tok = idx_ref[0, 0, r]
            pltpu.make_async_copy(src_ref.at[pl.ds(tok * s_per, s_per), :],
                                  out_ref.at[pl.ds(r * s_per, s_per), :], sem).start(priority=prio)
        return carry

    lax.fori_loop(0, rows // DMA_PRIORITIES, issue, 0, unroll=4)
    pltpu.make_async_copy(src_ref.at[pl.ds(0, rows * s_per), :], out_ref, sem).wait()


def _gather_rows(src, idx, d=D_MODEL, rows=GATHER_ROWS):
    n = idx.shape[0]
    rows = min(rows, n)
    assert n % rows == 0
    s_per = d // LANES
    return pl.pallas_call(
        functools.partial(_gather_kernel, rows=rows, s_per=s_per),
        grid=(n // rows,),
        in_specs=[pl.BlockSpec((1, 1, rows), lambda i: (i, 0, 0), memory_space=pltpu.SMEM),
                  pl.BlockSpec(memory_space=pl.ANY)],
        out_specs=pl.BlockSpec((rows * s_per, LANES), lambda i: (i, 0)),
        out_shape=jax.ShapeDtypeStruct((n * s_per, LANES), src.dtype),
        scratch_shapes=[pltpu.SemaphoreType.DMA],
        compiler_params=_cparams(("parallel",)),
        name="row_gather",
    )(idx.reshape(n // rows, 1, rows), src)


def _pack_bf16_pairs(h):
    half = h.shape[1] // 2
    bits = lax.bitcast_convert_type(h, jnp.uint32)
    rounded = bits + jnp.uint32(0x7FFF) + ((bits >> 16) & jnp.uint32(1))
    return (rounded[:, half:] & jnp.uint32(0xFFFF0000)) | (rounded[:, :half] >> 16)


def _unpack_bf16_pairs(words):
    lo = lax.bitcast_convert_type(words << 16, F32).astype(BF16)
    hi = lax.bitcast_convert_type(words & jnp.uint32(0xFFFF0000), F32).astype(BF16)
    return jnp.concatenate([lo, hi], axis=1)


def _moe_gu_kernel(te_ref, nv_ref, x_ref, wg_ref, wu_ref, bg_ref, bu_ref, o_ref):
    i = pl.program_id(1)

    @pl.when(i < nv_ref[0])
    def _():
        x = _unpack_bf16_pairs(_load_tiled(x_ref, 0, 1, d=x_ref.shape[0] // MOE_TM * LANES))
        g = jnp.dot(x, wg_ref[0, 0], preferred_element_type=F32) + bg_ref[0, 0]
        u = jnp.dot(x, wu_ref[0, 0], preferred_element_type=F32) + bu_ref[0, 0]
        g = jnp.minimum(g, SWIGLU_LIMIT)
        u = jnp.clip(u, -SWIGLU_LIMIT, SWIGLU_LIMIT)
        o_ref[...] = (g * jax.nn.sigmoid(SWIGLU_ALPHA * g) * (u + 1.0)).astype(o_ref.dtype)

    @pl.when(i >= nv_ref[0])
    def _():
        o_ref[...] = jnp.zeros_like(o_ref)


def _moe_down_kernel(te_ref, nv_ref, a_ref, wd_ref, bd_ref, o_ref):
    i = pl.program_id(0)

    @pl.when(i < nv_ref[0])
    def _():
        _store_tiled(o_ref, jnp.dot(a_ref[...], wd_ref[0, 0], preferred_element_type=F32) + bd_ref[0, 0])

    @pl.when(i >= nv_ref[0])
    def _():
        o_ref[...] = jnp.zeros_like(o_ref)


def _moe_experts(x_sorted, tile_expert, n_valid, w_gu, b_gu, w_d, b_d, layer):
    f, d = w_d.shape[2], w_d.shape[3]
    s_per = d // LANES
    s_in = s_per // 2
    n_slots = x_sorted.shape[0] // s_in
    tm, tn = MOE_TM, MOE_TN
    n_tiles = n_slots // tm
    nj = f // tn
    act = pl.pallas_call(
        _moe_gu_kernel,
        grid_spec=pltpu.PrefetchScalarGridSpec(
            num_scalar_prefetch=2,
            grid=(nj, n_tiles),
            in_specs=[pl.BlockSpec((tm * s_in, LANES), lambda j, i, te, nv: (i, 0)),
                      pl.BlockSpec((1, 1, d, tn), lambda j, i, te, nv: (layer, te[i], 0, j)),
                      pl.BlockSpec((1, 1, d, tn), lambda j, i, te, nv: (layer, te[i], 0, nj + j)),
                      pl.BlockSpec((1, 1, 1, tn), lambda j, i, te, nv: (layer, te[i], 0, j)),
                      pl.BlockSpec((1, 1, 1, tn), lambda j, i, te, nv: (layer, te[i], 0, nj + j))],
            out_specs=pl.BlockSpec((tm, tn), lambda j, i, te, nv: (i, j)),
        ),
        out_shape=jax.ShapeDtypeStruct((n_slots, f), BF16),
        compiler_params=_cparams(("arbitrary", "arbitrary")),
        name="moe_gate_up",
    )(tile_expert, n_valid, x_sorted, w_gu, w_gu, b_gu, b_gu)
    y = pl.pallas_call(
        _moe_down_kernel,
        grid_spec=pltpu.PrefetchScalarGridSpec(
            num_scalar_prefetch=2,
            grid=(n_tiles,),
            in_specs=[pl.BlockSpec((tm, f), lambda i, te, nv: (i, 0)),
                      pl.BlockSpec((1, 1, f, d), lambda i, te, nv: (layer, te[i], 0, 0)),
                      pl.BlockSpec((1, 1, 1, d), lambda i, te, nv: (layer, te[i], 0, 0))],
            out_specs=pl.BlockSpec((tm * s_per, LANES), lambda i, te, nv: (i, 0)),
        ),
        out_shape=jax.ShapeDtypeStruct((n_slots * s_per, LANES), F32),
        compiler_params=_cparams(("arbitrary",)),
        name="moe_down",
    )(tile_expert, n_valid, act, w_d, b_d)
    return y


def _moe_plan(top_idx):
    t = top_idx.shape[0]
    n_items = t * MOE_TOPK
    n_slots = n_items + N_EXPERTS * MOE_TM
    items = top_idx.reshape(n_items)
    onehot = (items[:, None] == jnp.arange(N_EXPERTS, dtype=jnp.int32)[None, :]).astype(jnp.int32)
    csum = jnp.cumsum(onehot, axis=0)
    counts = csum[-1]
    rank = jnp.sum(csum * onehot, axis=1) - 1
    padded = (counts + MOE_TM - 1) // MOE_TM * MOE_TM
    pad_end = jnp.cumsum(padded)
    pad_start = pad_end - padded
    dest = (pad_start[items] + rank).astype(jnp.int32)
    slot_tok = (jnp.arange(n_slots, dtype=jnp.int32) % t).at[dest].set(
        jnp.arange(n_items, dtype=jnp.int32) // MOE_TOPK)
    tile_start = jnp.arange(n_slots // MOE_TM, dtype=jnp.int32) * MOE_TM
    tile_expert = jnp.minimum(jnp.searchsorted(pad_end, tile_start, side='right'), N_EXPERTS - 1).astype(jnp.int32)
    n_valid = (pad_end[-1] // MOE_TM).astype(jnp.int32).reshape(1)
    return dest, slot_tok, tile_expert, n_valid


def _rope(x, cos_f, sin_s):
    lane = lax.broadcasted_iota(jnp.int32, x.shape, 1)
    half = ROPE_DIM // 2
    partner = jnp.where(lane < half, pltpu.roll(x, LANES - half, 1), pltpu.roll(x, half, 1))
    return x * cos_f + partner * sin_s


def _moba_kernel(q_ref, k_ref, v_ref, cos_ref, sin_ref, o_ref, *, n_blk):
    cos_f, sin_s = cos_ref[...], sin_ref[...]
    for hh in range(MOBA_HEADS_PER_STEP):
        cols = slice(hh * HEAD_DIM, (hh + 1) * HEAD_DIM)
        o_ref[:, cols] = _moba_head(q_ref[:, cols], k_ref[:, cols], v_ref[:, cols], cos_f, sin_s,
                                    n_blk).astype(o_ref.dtype)


def _moba_head(q, k, v, cos_f, sin_s, n_blk):
    blk = MOBA_BLOCK
    kr = _rope(k, cos_f, sin_s)
    means = [jnp.mean(kr[j * blk:(j + 1) * blk], axis=0, keepdims=True) for j in range(n_blk)]
    n_rows = max(n_blk, 8)
    if n_rows > n_blk:
        means.append(jnp.zeros((n_rows - n_blk, HEAD_DIM), F32))
    kmean = jnp.concatenate(means, axis=0).astype(BF16)
    ks = (kr * (HEAD_DIM ** -0.5)).astype(BF16)
    vt = v.T.astype(BF16)
    qt = _rope(q, cos_f, sin_s).T.astype(BF16)

    gate = jnp.dot(kmean, qt, preferred_element_type=F32)
    row_i = lax.broadcasted_iota(jnp.int32, gate.shape, 0)
    rowf = row_i.astype(F32)
    past = row_i < lax.broadcasted_iota(jnp.int32, gate.shape, 1) // blk
    cur = jnp.where(past, gate, NEG_INF)
    sel = jnp.zeros_like(gate)
    for _ in range(min(MOBA_TOPK, n_blk - 1)):
        m = jnp.max(cur, axis=0, keepdims=True)
        idx = jnp.min(jnp.where(cur == m, rowf, float(LANES)), axis=0, keepdims=True)
        hit = rowf == idx
        sel = jnp.where(hit & past, 1.0, sel)
        cur = jnp.where(hit, -3.0e38, cur)

    r_i = lax.broadcasted_iota(jnp.int32, (blk, blk), 0)
    c_i = lax.broadcasted_iota(jnp.int32, (blk, blk), 1)
    outs = []
    for qb in range(n_blk):
        cols = slice(qb * blk, (qb + 1) * blk)
        nk = (qb + 1) * blk
        s = jnp.dot(ks[:nk], qt[:, cols], preferred_element_type=F32)
        parts = [jnp.where(sel[j:j + 1, cols] > 0.0, s[j * blk:(j + 1) * blk], NEG_INF) for j in range(qb)]
        parts.append(jnp.where(r_i <= c_i, s[qb * blk:], NEG_INF))
        s = jnp.concatenate(parts, axis=0)
        p = jnp.exp(s - jnp.max(s, axis=0, keepdims=True))
        l = jnp.sum(p, axis=0, keepdims=True)
        acc = jnp.dot(vt[:, :nk], p.astype(BF16), preferred_element_type=F32)
        outs.append(acc / l)
    return jnp.concatenate(outs, axis=1).T


def _moba_attention(qkv, cos_f, sin_s, batch, seq):
    t = qkv.shape[0]
    n_blk = seq // MOBA_BLOCK
    h = MOBA_HEADS // MOBA_HEADS_PER_STEP
    head = lambda off: pl.BlockSpec((seq, MOBA_HEADS_PER_STEP * HEAD_DIM), lambda b, hh: (b, off + hh))
    table = pl.BlockSpec((seq, HEAD_DIM), lambda b, hh: (b, 0))
    return pl.pallas_call(
        functools.partial(_moba_kernel, n_blk=n_blk),
        grid=(batch, h),
        in_specs=[head(0), head(h), head(2 * h), table, table],
        out_specs=head(0),
        out_shape=jax.ShapeDtypeStruct((t, MOBA_HEADS * HEAD_DIM), BF16),
        compiler_params=_cparams(("parallel", "parallel")),
        name="moba_attention",
    )(qkv, qkv, qkv, cos_f, sin_s)


def _rope_tables(positions):
    half = ROPE_DIM // 2
    inv_freq = ROPE_THETA ** (-jnp.arange(0, ROPE_DIM, 2, dtype=F32) / ROPE_DIM)
    ang = positions.astype(F32).reshape(-1, 1) * inv_freq
    cos, sin = jnp.cos(ang), jnp.sin(ang)
    t = ang.shape[0]
    rest = HEAD_DIM - ROPE_DIM
    cos_f = jnp.concatenate([cos, cos, jnp.ones((t, rest), F32)], axis=1)
    sin_s = jnp.concatenate([-sin, sin, jnp.zeros((t, rest), F32)], axis=1)
    return cos_f, sin_s


def _shift_rows(x, k):
    rolled = pltpu.roll(x, k, 0)
    row = lax.broadcasted_iota(jnp.int32, (8, x.shape[1]), 0)
    return jnp.concatenate([jnp.where(row >= k, rolled[0:8], 0.0), rolled[8:]], axis=0)


def _conv_silu(x, w):
    y = x * w[GDN_CONV - 1:GDN_CONV, :]
    for j in range(GDN_CONV - 1):
        y = y + _shift_rows(x, GDN_CONV - 1 - j) * w[j:j + 1, :]
    return y * jax.nn.sigmoid(y)


def _l2_normalize(y):
    return y * lax.rsqrt(jnp.sum(y * y, axis=1, keepdims=True) + L2_EPS)


def _gdn_gates_kernel(ba_ref, alog_ref, dtb_ref, gb_ref, gbt_ref):
    x = ba_ref[...]
    lane = lax.broadcasted_iota(jnp.int32, x.shape, 1)
    row = lax.broadcasted_iota(jnp.int32, x.shape, 0)
    xa = x + dtb_ref[...]
    softplus = jnp.maximum(xa, 0.0) + jnp.log(1.0 + jnp.exp(-jnp.abs(xa)))
    g = -jnp.exp(alog_ref[...]) * softplus
    pos = row % GDN_CHUNK
    sh = 1
    while sh < GDN_CHUNK:
        g = g + jnp.where(pos >= sh, pltpu.roll(g, sh, 0), 0.0)
        sh *= 2
    out = jnp.where(lane < GDN_V_HEADS, jax.nn.sigmoid(x), g)
    gb_ref[...] = out
    gbt_ref[...] = out.T


def _gdn_gates(ba, a_log, dt_bias, batch, seq):
    pad = jnp.zeros((LANES - 2 * GDN_V_HEADS,), F32)
    zeros = jnp.zeros((GDN_V_HEADS,), F32)
    alog_row = jnp.concatenate([zeros, a_log.astype(F32), pad]).reshape(1, LANES)
    dtb_row = jnp.concatenate([zeros, dt_bias.astype(F32), pad]).reshape(1, LANES)
    return pl.pallas_call(
        _gdn_gates_kernel,
        grid=(batch,),
        in_specs=[pl.BlockSpec((seq, LANES), lambda b: (b, 0)),
                  pl.BlockSpec((1, LANES), lambda b: (0, 0)),
                  pl.BlockSpec((1, LANES), lambda b: (0, 0))],
        out_specs=[pl.BlockSpec((seq, LANES), lambda b: (b, 0)),
                   pl.BlockSpec((LANES, seq), lambda b: (b, 0))],
        out_shape=[jax.ShapeDtypeStruct((batch * seq, LANES), F32),
                   jax.ShapeDtypeStruct((batch * LANES, seq), F32)],
        compiler_params=_cparams(("parallel",)),
        name="gdn_gates",
    )(ba, alog_row, dtb_row)


def _bmm(a, b):
    return jnp.einsum('nij,njk->nik', a.astype(BF16), b.astype(BF16), preferred_element_type=F32)


def _gdn_core_kernel(q_ref, k_ref, v_ref, z_ref, cwq_ref, cwk_ref, cwv_ref, gb_ref, gbt_ref, nw_ref, o_ref,
                     b_s, c_s, qe_s, o0_s, eg_s, st_s, *, n_chunks):
    c = GDN_CHUNK
    hk = pl.program_id(1)
    q2 = _l2_normalize(_conv_silu(q_ref[...], cwq_ref[...]))
    k2 = _l2_normalize(_conv_silu(k_ref[...], cwk_ref[...]))
    v2 = _conv_silu(v_ref[...], cwv_ref[...])
    q3 = (q2 * (HEAD_DIM ** -0.5)).reshape(n_chunks, c, HEAD_DIM)
    k3 = k2.reshape(n_chunks, c, HEAD_DIM)
    kt = k2.T
    qh, kh = q3.astype(BF16), k3.astype(BF16)
    kk = jnp.einsum('nid,njd->nij', kh, kh, preferred_element_type=F32)
    qk = jnp.einsum('nid,njd->nij', qh, kh, preferred_element_type=F32)
    r_i = lax.broadcasted_iota(jnp.int32, (c, c), 0)
    c_i = lax.broadcasted_iota(jnp.int32, (c, c), 1)
    tril = c_i <= r_i
    strict = c_i < r_i
    eye = (c_i == r_i).astype(F32)
    lane = lax.broadcasted_iota(jnp.int32, gb_ref.shape, 1)
    gb = gb_ref[...]

    for vh in range(2):
        hv = 2 * hk + vh
        beta = jnp.sum(jnp.where(lane == hv, gb, 0.0), axis=1, keepdims=True).reshape(n_chunks, c, 1)
        gcol = jnp.sum(jnp.where(lane == GDN_V_HEADS + hv, gb, 0.0), axis=1, keepdims=True).reshape(n_chunks, c, 1)
        grow = gbt_ref[pl.ds(GDN_V_HEADS + hv, 1), :]
        decay, kdt = [], []
        for n in range(n_chunks):
            gr = grow[:, n * c:(n + 1) * c]
            diff = gcol[n] - gr
            decay.append(jnp.exp(jnp.where(tril, diff, NEG_INF)))
            kdt.append(kt[:, n * c:(n + 1) * c] * jnp.exp(gr[:, c - 1:c] - gr))
        decay = jnp.stack(decay, axis=0)
        kdt = jnp.stack(kdt, axis=0)
        m = jnp.where(strict, beta * kk * decay, 0.0)
        attn = jnp.where(tril, qk * decay, 0.0)
        t = eye - jnp.where((r_i // 2 == c_i // 2), m, 0.0)
        mb = m.astype(BF16)
        s = 2
        while s < c:
            off = (r_i // (2 * s) == c_i // (2 * s)) & ((r_i % (2 * s)) >= s) & ((c_i % (2 * s)) < s)
            moff = jnp.where(off, mb, jnp.zeros_like(mb))
            tb = t.astype(BF16)
            t = t - _bmm(_bmm(tb, moff), tb)
            s *= 2
        eg = jnp.exp(gcol)
        v3 = v2[:, vh * HEAD_DIM:(vh + 1) * HEAD_DIM].reshape(n_chunks, c, HEAD_DIM)
        uw = _bmm(t, jnp.concatenate([v3 * beta, k3 * (beta * eg)], axis=-1))
        bc = _bmm(kdt, uw)
        ao = _bmm(attn, uw)
        b_s[vh] = bc[:, :, :HEAD_DIM]
        c_s[vh] = bc[:, :, HEAD_DIM:].astype(BF16)
        o0_s[vh] = ao[:, :, :HEAD_DIM]
        qe_s[vh] = (q3 * eg - ao[:, :, HEAD_DIM:]).astype(BF16)
        glast = gcol[:, c - 1:c, :]
        eg_s[vh] = jnp.broadcast_to(jnp.exp(glast), (n_chunks, 8, HEAD_DIM))
        st_s[vh] = jnp.zeros((HEAD_DIM, HEAD_DIM), F32)

    nw = nw_ref[...]

    def chunk_step(n, carry):
        rows = pl.ds(pl.multiple_of(n * c, c), c)
        for vh in range(2):
            state = st_s[vh]
            sb = state.astype(BF16)
            o = jnp.dot(qe_s[vh, n], sb, preferred_element_type=F32) + o0_s[vh, n]
            st_s[vh] = (state * eg_s[vh, n][0:1, :] + b_s[vh, n]
                        - jnp.dot(c_s[vh, n], sb, preferred_element_type=F32))
            zz = z_ref[rows, vh * HEAD_DIM:(vh + 1) * HEAD_DIM]
            on = o * lax.rsqrt(jnp.mean(o * o, axis=1, keepdims=True) + GDN_NORM_EPS)
            o_ref[rows, vh * HEAD_DIM:(vh + 1) * HEAD_DIM] = (on * nw * (zz * jax.nn.sigmoid(zz))).astype(o_ref.dtype)
        return carry

    lax.fori_loop(0, n_chunks, chunk_step, 0)


def _gdn_core(proj, conv_w, gb, gbt, norm_w, batch, seq):
    n_chunks = seq // GDN_CHUNK
    hd = HEAD_DIM
    vblk0 = 2 * GDN_KEY_DIM // (2 * hd)
    zblk0 = GDN_CONV_DIM // (2 * hd)
    big = lambda dt: pltpu.VMEM((2, n_chunks, GDN_CHUNK, hd), dt)
    return pl.pallas_call(
        functools.partial(_gdn_core_kernel, n_chunks=n_chunks),
        grid=(batch, GDN_K_HEADS),
        in_specs=[pl.BlockSpec((seq, hd), lambda b, h: (b, h)),
                  pl.BlockSpec((seq, hd), lambda b, h: (b, GDN_K_HEADS + h)),
                  pl.BlockSpec((seq, 2 * hd), lambda b, h: (b, vblk0 + h)),
                  pl.BlockSpec((seq, 2 * hd), lambda b, h: (b, zblk0 + h)),
                  pl.BlockSpec((GDN_CONV, hd), lambda b, h: (0, h)),
                  pl.BlockSpec((GDN_CONV, hd), lambda b, h: (0, GDN_K_HEADS + h)),
                  pl.BlockSpec((GDN_CONV, 2 * hd), lambda b, h: (0, vblk0 + h)),
                  pl.BlockSpec((seq, LANES), lambda b, h: (b, 0)),
                  pl.BlockSpec((LANES, seq), lambda b, h: (b, 0)),
                  pl.BlockSpec((1, hd), lambda b, h: (0, 0))],
        out_specs=pl.BlockSpec((seq, 2 * hd), lambda b, h: (b, h)),
        out_shape=jax.ShapeDtypeStruct((batch * seq, GDN_VALUE_DIM), BF16),
        scratch_shapes=[big(F32), big(BF16), big(BF16), big(F32),
                        pltpu.VMEM((2, n_chunks, 8, hd), F32), pltpu.VMEM((2, hd, hd), F32)],
        compiler_params=_cparams(("parallel", "arbitrary")),
        name="gdn_core",
    )(proj, proj, proj, proj, conv_w, conv_w, conv_w, gb, gbt, norm_w.reshape(1, hd).astype(F32))


def _gated_deltanet(h, w_in, conv_w, a_log, dt_bias, norm_w, w_out, batch, seq):
    w_main = w_in[:, :GDN_MAIN_DIM].astype(BF16)
    w_ba = jnp.pad(w_in[:, GDN_MAIN_DIM:], ((0, 0), (0, LANES - 2 * GDN_V_HEADS))).astype(BF16)
    proj = _matmul(h, w_main, out_dtype=F32, tm=1024, tn=1024, name="gdn_in_proj")
    ba = _matmul(h, w_ba, out_dtype=F32, tm=1024, tn=LANES, name="gdn_ba_proj")
    gb, gbt = _gdn_gates(ba, a_log, dt_bias, batch, seq)
    o = _gdn_core(proj, conv_w, gb, gbt, norm_w, batch, seq)
    return _matmul(o, w_out.astype(BF16), out_dtype=F32, tm=512, tn=1024, name="gdn_out_proj")


def _adaln(c, w, b):
    m = _matmul(c, w.astype(BF16), out_dtype=F32, tm=c.shape[0], tn=1024, bias=b, silu_a=True, name="adaln")
    d = w.shape[0]
    return m[:, :d], m[:, d:2 * d], m[:, 2 * d:]


def _moe_sublayer(x, h, top_idx, gate_w, gate_c, lng, lnb, seq, moe_w, layer, nxt):
    dest, slot_tok, tile_expert, n_valid = _moe_plan(top_idx[:, :MOE_TOPK])
    x_sorted = _gather_rows(h, slot_tok, d=D_MODEL // 2)
    y_sorted = _moe_experts(x_sorted, tile_expert, n_valid, *moe_w, layer)
    return _post(x, y_sorted, gate_c, lng, lnb, seq, w4=(gate_w, dest), nxt=nxt)


def kernel(x, c, positions, ada_w, ada_b, ln_g, ln_b, moba_w_qkv, moba_w_o, gdn_w_in, gdn_conv_w, gdn_a_log,
           gdn_dt_bias, gdn_norm_w, gdn_w_out, router_w, router_b, moe_w_gate_up, moe_b_gate_up, moe_w_down,
           moe_b_down):
    batch, seq, d = x.shape
    t = batch * seq
    xs = x.reshape(t, d)
    cos_f, sin_s = _rope_tables(positions)
    mods = [[_adaln(c, ada_w[i, j], ada_b[i, j]) for j in range(2)] for i in range(DEPTH)]
    moe_w = (moe_w_gate_up.astype(BF16), moe_b_gate_up[:, :, None, :], moe_w_down.astype(BF16),
             moe_b_down[:, :, None, :])

    def router(i):
        wr = jnp.pad(router_w[i], ((0, 0), (0, LANES - N_EXPERTS)))
        br = jnp.concatenate([router_b[i], jnp.full((LANES - N_EXPERTS,), NEG_INF, F32)]).reshape(1, LANES)
        return wr, br

    for i in range(DEPTH):
        shift, scale, gate = mods[i][0]
        if i == 0:
            h = _modulate(xs, scale, shift, seq, BF16)
        if i % 2 == 0:
            qkv = _matmul(h, moba_w_qkv[i // 2].astype(BF16), out_dtype=F32, tm=1024, tn=1024, name="moba_qkv")
            o = _moba_attention(qkv, cos_f, sin_s, batch, seq)
            y = _matmul(o, moba_w_o[i // 2].astype(BF16), out_dtype=F32, tm=1024, tn=1024, name="moba_out")
        else:
            j = i // 2
            y = _gated_deltanet(h, gdn_w_in[j], gdn_conv_w[j], gdn_a_log[j], gdn_dt_bias[j], gdn_norm_w[j],
                                gdn_w_out[j], batch, seq)
        shift2, scale2, gate2 = mods[i][1]
        xs, h32, top_idx, gate_w = _post(xs, y, gate, ln_g[i, 0], ln_b[i, 0], seq, nxt=(scale2, shift2),
                                         router=router(i), h_tiled=True)
        nxt = None
        if i + 1 < DEPTH:
            shift_n, scale_n, _ = mods[i + 1][0]
            nxt = (scale_n, shift_n)
        res = _moe_sublayer(xs, h32, top_idx, gate_w, gate2, ln_g[i, 1], ln_b[i, 1], seq, moe_w, i, nxt)
        xs = res[0]
        if nxt is not None:
            h = res[1]
    return xs.reshape(batch, seq, d)
```

```python
import functools

import jax
import jax.numpy as jnp
from jax import lax
from jax.experimental import pallas as pl
from jax.experimental.pallas import tpu as pltpu

F32 = jnp.float32
BF16 = jnp.bfloat16

D_MODEL = 2048
DEPTH = 2
DEEPNORM_ALPHA = float((2 * DEPTH) ** 0.25)
LN_EPS = 1e-5
MOBA_HEADS = 16
HEAD_DIM = 128
MOBA_BLOCK = 256
MOBA_TOPK = 3
MOBA_HEADS_PER_STEP = 2
ROPE_THETA = 500000.0
ROPE_DIM = HEAD_DIM // 4
NEG_INF = -1e30
GDN_K_HEADS = 16
GDN_V_HEADS = 32
GDN_KEY_DIM = GDN_K_HEADS * HEAD_DIM
GDN_VALUE_DIM = GDN_V_HEADS * HEAD_DIM
GDN_CONV_DIM = 2 * GDN_KEY_DIM + GDN_VALUE_DIM
GDN_MAIN_DIM = GDN_CONV_DIM + GDN_VALUE_DIM
GDN_CONV = 4
GDN_CHUNK = 128
GDN_NORM_EPS = 1e-6
L2_EPS = 1e-6
N_EXPERTS = 32
MOE_TOPK = 4
SWIGLU_ALPHA = 1.702
SWIGLU_LIMIT = 7.0

LANES = 128
VMEM_LIMIT = 50 * 1024 * 1024
DMA_PRIORITIES = 2
COMBINE_PITCH = 20

MOE_TM = 512
MOE_TN = 1024
DISPATCH_ROWS = 1024
POST_TM = 256


def _cparams(sem):
    return pltpu.CompilerParams(dimension_semantics=sem, vmem_limit_bytes=VMEM_LIMIT)


def _mm_kernel(*refs, silu_a, has_bias):
    if has_bias:
        a_ref, b_ref, bias_ref, o_ref = refs
    else:
        a_ref, b_ref, o_ref = refs
    a = a_ref[...]
    if silu_a:
        a = a.astype(F32)
        a = a * jax.nn.sigmoid(a)
    acc = jnp.dot(a.astype(BF16), b_ref[...], preferred_element_type=F32)
    if has_bias:
        acc = acc + bias_ref[...]
    o_ref[...] = acc.astype(o_ref.dtype)


def _matmul(a, b, *, out_dtype, tm, tn, bias=None, silu_a=False, name="mm"):
    m, k = a.shape
    n = b.shape[1]
    tm = min(tm, m)
    tn = min(tn, n)
    assert m % tm == 0 and n % tn == 0, (m, n, tm, tn)
    in_specs = [pl.BlockSpec((tm, k), lambda i, j: (i, 0)),
                pl.BlockSpec((k, tn), lambda i, j: (0, j))]
    args = [a, b]
    if bias is not None:
        in_specs.append(pl.BlockSpec((1, tn), lambda i, j: (0, j)))
        args.append(bias.reshape(1, n).astype(F32))
    return pl.pallas_call(
        functools.partial(_mm_kernel, silu_a=silu_a, has_bias=bias is not None),
        grid=(m // tm, n // tn),
        in_specs=in_specs,
        out_specs=pl.BlockSpec((tm, tn), lambda i, j: (i, j)),
        out_shape=jax.ShapeDtypeStruct((m, n), out_dtype),
        compiler_params=_cparams(("parallel", "parallel")),
        name=name,
    )(*args)


def _modulate_kernel(x_ref, scale_ref, shift_ref, h_ref):
    h = x_ref[...] * (1.0 + scale_ref[0]) + shift_ref[0]
    h_ref[...] = h.astype(h_ref.dtype)


def _modulate(x, scale, shift, seq, out_dtype):
    t, d = x.shape
    tm = POST_TM
    per_b = seq // tm
    vec = pl.BlockSpec((1, 1, d), lambda i: (i // per_b, 0, 0))
    return pl.pallas_call(
        _modulate_kernel,
        grid=(t // tm,),
        in_specs=[pl.BlockSpec((tm, d), lambda i: (i, 0)), vec, vec],
        out_specs=pl.BlockSpec((tm, d), lambda i: (i, 0)),
        out_shape=jax.ShapeDtypeStruct((t, d), out_dtype),
        compiler_params=_cparams(("parallel",)),
        name="modulate",
    )(x, scale[:, None, :], shift[:, None, :])


def _top4(logits):
    lane = lax.broadcasted_iota(jnp.int32, logits.shape, 1).astype(F32)
    cur = logits
    vals, idxs = [], []
    for _ in range(MOE_TOPK):
        m = jnp.max(cur, axis=1, keepdims=True)
        idx = jnp.min(jnp.where(cur == m, lane, float(LANES)), axis=1, keepdims=True)
        vals.append(m)
        idxs.append(idx)
        cur = jnp.where(lane == idx, -3.0e38, cur)
    exps = [jnp.exp(v - vals[0]) for v in vals]
    denom = exps[0] + exps[1] + exps[2] + exps[3]
    idx_out = jnp.zeros_like(logits)
    gate_out = jnp.zeros_like(logits)
    for k in range(MOE_TOPK):
        idx_out = jnp.where(lane == float(k), idxs[k], idx_out)
        gate_out = jnp.where(lane == float(k), exps[k] / denom, gate_out)
    return idx_out.astype(jnp.int32), gate_out


def _store_tiled(ref, val, pitch=None):
    n, d = val.shape
    s_per = d // LANES
    pitch = s_per if pitch is None else pitch
    for s in range(s_per):
        ref[pl.ds(s, n, stride=pitch), :] = val[:, s * LANES:(s + 1) * LANES]
    for s in range(s_per, pitch):
        ref[pl.ds(s, n, stride=pitch), :] = jnp.zeros((n, LANES), val.dtype)


def _load_tiled(ref, member, group, d=D_MODEL):
    s_per = d // LANES
    n = ref.shape[0] // (group * s_per)
    return jnp.concatenate([ref[pl.ds(member * s_per + s, n, stride=group * s_per), :] for s in range(s_per)],
                           axis=1)


def _post_kernel(*refs, moe_in, has_next, has_router, h_tiled, d):
    it = iter(refs)
    if moe_in:
        cur_idx_ref = next(it)
        nxt_idx_ref = next(it)
    x_ref = next(it)
    y_ref = next(it)
    w4_ref = next(it) if moe_in else None
    gate_ref = next(it)
    lng_ref = next(it)
    lnb_ref = next(it)
    if has_next:
        scale_ref = next(it)
        shift_ref = next(it)
    if has_router:
        wr_ref = next(it)
        br_ref = next(it)
    xo_ref = next(it)
    if has_next:
        h_ref = next(it)
    if has_router:
        idx_ref = next(it)
        gw_ref = next(it)
    if moe_in:
        ybuf = next(it)
        sems = next(it)

    if moe_in:
        i = pl.program_id(0)
        tm = x_ref.shape[0]
        n_rows = tm * MOE_TOPK
        s_per = d // LANES

        def start_rows(src_idx_ref, slot):
            def issue(tok, carry):
                for k in range(MOE_TOPK):
                    row = src_idx_ref[0, 0, tok * MOE_TOPK + k]
                    pltpu.make_async_copy(y_ref.at[pl.ds(row * COMBINE_PITCH, s_per), :],
                                          ybuf.at[slot, pl.ds((k * tm + tok) * COMBINE_PITCH, s_per), :],
                                          sems.at[slot]).start(priority=k % DMA_PRIORITIES)
                return carry
            lax.fori_loop(0, tm, issue, 0, unroll=2)

        @pl.when(i == 0)
        def _():
            start_rows(cur_idx_ref, 0)

        @pl.when(i + 1 < pl.num_programs(0))
        def _():
            start_rows(nxt_idx_ref, (i + 1) % 2)

        slot = i % 2
        pltpu.make_async_copy(y_ref.at[pl.ds(0, n_rows * s_per), :],
                              ybuf.at[slot, pl.ds(0, n_rows * s_per), :], sems.at[slot]).wait()
        yb = ybuf.at[slot]

        def expert_rows(k):
            return jnp.concatenate([yb[pl.ds(k * tm * COMBINE_PITCH + s, tm, stride=COMBINE_PITCH), :]
                                    for s in range(s_per)], axis=1)

        w4 = w4_ref[...]
        y = w4[:, 0:1] * expert_rows(0)
        for k in range(1, MOE_TOPK):
            y = y + w4[:, k:k + 1] * expert_rows(k)
    else:
        y = y_ref[...].astype(F32)
    z = DEEPNORM_ALPHA * x_ref[...] + (1.0 + gate_ref[0]) * y
    mu = jnp.mean(z, axis=1, keepdims=True)
    zc = z - mu
    var = jnp.mean(zc * zc, axis=1, keepdims=True)
    xn = zc * lax.rsqrt(var + LN_EPS) * lng_ref[...] + lnb_ref[...]
    xo_ref[...] = xn
    if has_next:
        h = xn * (1.0 + scale_ref[0]) + shift_ref[0]
        if h_tiled:
            _store_tiled(h_ref, _pack_bf16_pairs(h))
        else:
            h_ref[...] = h.astype(h_ref.dtype)
    if has_router:
        logits = jnp.dot(h, wr_ref[...], precision=lax.Precision.HIGHEST,
                         preferred_element_type=F32) + br_ref[...]
        idx, gw = _top4(logits)
        idx_ref[...] = idx
        gw_ref[...] = gw


def _post(x, y, gate, lng, lnb, seq, *, w4=None, nxt=None, router=None, h_tiled=False):
    t, d = x.shape
    tm = POST_TM
    per_b = seq // tm
    s_per = d // LANES
    row = pl.BlockSpec((tm, d), lambda i: (i, 0))
    vec = pl.BlockSpec((1, 1, d), lambda i: (i // per_b, 0, 0))
    full = lambda a: pl.BlockSpec(a.shape, lambda i: (0,) * a.ndim)
    moe_in = w4 is not None
    n_steps = t // tm
    scratch = []
    if moe_in:
        w4, dest = w4
        n_rows = tm * MOE_TOPK
        dest3 = dest.reshape(n_steps, 1, n_rows)
        smem = lambda f: pl.BlockSpec((1, 1, n_rows), f, memory_space=pltpu.SMEM)
        args = [dest3, dest3, x, y, w4]
        in_specs = [smem(lambda i: (i, 0, 0)), smem(lambda i: (jnp.minimum(i + 1, n_steps - 1), 0, 0)),
                    row, pl.BlockSpec(memory_space=pl.ANY), pl.BlockSpec((tm, LANES), lambda i: (i, 0))]
        scratch = [pltpu.VMEM((2, n_rows * COMBINE_PITCH, LANES), F32), pltpu.SemaphoreType.DMA((2,))]
    else:
        args = [x, y]
        in_specs = [row, row]
    lng2, lnb2 = lng.reshape(1, d), lnb.reshape(1, d)
    args += [gate[:, None, :], lng2, lnb2]
    in_specs += [vec, full(lng2), full(lnb2)]
    out_shape = [jax.ShapeDtypeStruct((t, d), F32)]
    out_specs = [row]
    if nxt is not None:
        args += [nxt[0][:, None, :], nxt[1][:, None, :]]
        in_specs += [vec, vec]
        if h_tiled:
            out_shape.append(jax.ShapeDtypeStruct((t * s_per // 2, LANES), jnp.uint32))
            out_specs.append(pl.BlockSpec((tm * s_per // 2, LANES), lambda i: (i, 0)))
        else:
            out_shape.append(jax.ShapeDtypeStruct((t, d), BF16))
            out_specs.append(row)
    if router is not None:
        wr, br = router
        args += [wr, br]
        in_specs += [full(wr), full(br)]
        lane_row = pl.BlockSpec((tm, LANES), lambda i: (i, 0))
        out_shape += [jax.ShapeDtypeStruct((t, LANES), jnp.int32), jax.ShapeDtypeStruct((t, LANES), F32)]
        out_specs += [lane_row, lane_row]
    return pl.pallas_call(
        functools.partial(_post_kernel, moe_in=moe_in, has_next=nxt is not None,
                          has_router=router is not None, h_tiled=h_tiled, d=d),
        grid=(n_steps,),
        in_specs=in_specs,
        out_specs=out_specs,
        out_shape=out_shape,
        scratch_shapes=scratch,
        compiler_params=_cparams(("arbitrary",) if moe_in else ("parallel",)),
        name="moe_combine_post" if moe_in else "deepnorm_post",
    )(*args)


def _dispatch_kernel(idx_ref, src_ref, zero_ref, out_ref, sem, *, rows, s_per):
    def issue(tok, carry):
        for k in range(MOE_TOPK):
            slot = idx_ref[0, 0, tok * MOE_TOPK + k]
            pltpu.make_async_copy(src_ref.at[pl.ds(tok * s_per, s_per), :],
                                  out_ref.at[pl.ds(slot * s_per, s_per), :], sem).start(priority=k % DMA_PRIORITIES)
        return carry

    lax.fori_loop(0, rows, issue, 0, unroll=2)
    for _ in range(MOE_TOPK):
        pltpu.make_async_copy(src_ref, out_ref.at[pl.ds(0, rows * s_per), :], sem).wait()


def _dispatch_rows(src, dest, n_slots, d=D_MODEL // 2, rows=DISPATCH_ROWS):
    s_per = d // LANES
    n = src.shape[0] // s_per
    rows = min(rows, n)
    assert n % rows == 0
    zeros = jnp.zeros((n_slots * s_per, LANES), src.dtype)
    return pl.pallas_call(
        functools.partial(_dispatch_kernel, rows=rows, s_per=s_per),
        grid=(n // rows,),
        in_specs=[pl.BlockSpec((1, 1, rows * MOE_TOPK), lambda i: (i, 0, 0), memory_space=pltpu.SMEM),
                  pl.BlockSpec((rows * s_per, LANES), lambda i: (i, 0)),
                  pl.BlockSpec(memory_space=pl.ANY)],
        out_specs=pl.BlockSpec(memory_space=pl.ANY),
        out_shape=jax.ShapeDtypeStruct((n_slots * s_per, LANES), src.dtype),
        input_output_aliases={2: 0},
        scratch_shapes=[pltpu.SemaphoreType.DMA],
        compiler_params=_cparams(("arbitrary",)),
        name="row_dispatch",
    )(dest.reshape(n // rows, 1, rows * MOE_TOPK), src, zeros)


def _pack_bf16_pairs(h):
    half = h.shape[1] // 2
    bits = lax.bitcast_convert_type(h, jnp.uint32)
    rounded = bits + jnp.uint32(0x7FFF) + ((bits >> 16) & jnp.uint32(1))
    return (rounded[:, half:] & jnp.uint32(0xFFFF0000)) | (rounded[:, :half] >> 16)


def _unpack_bf16_pairs(words):
    lo = lax.bitcast_convert_type(words << 16, F32).astype(BF16)
    hi = lax.bitcast_convert_type(words & jnp.uint32(0xFFFF0000), F32).astype(BF16)
    return jnp.concatenate([lo, hi], axis=1)


def _moe_gu_kernel(te_ref, nv_ref, x_ref, wg_ref, wu_ref, bg_ref, bu_ref, o_ref):
    i = pl.program_id(1)

    @pl.when(i < nv_ref[0])
    def _():
        x = _unpack_bf16_pairs(_load_tiled(x_ref, 0, 1, d=x_ref.shape[0] // MOE_TM * LANES))
        g = jnp.dot(x, wg_ref[0, 0], preferred_element_type=F32) + bg_ref[0, 0]
        u = jnp.dot(x, wu_ref[0, 0], preferred_element_type=F32) + bu_ref[0, 0]
        g = jnp.minimum(g, SWIGLU_LIMIT)
        u = jnp.clip(u, -SWIGLU_LIMIT, SWIGLU_LIMIT)
        o_ref[...] = (g * jax.nn.sigmoid(SWIGLU_ALPHA * g) * (u + 1.0)).astype(o_ref.dtype)

    @pl.when(i >= nv_ref[0])
    def _():
        o_ref[...] = jnp.zeros_like(o_ref)


def _moe_down_kernel(te_ref, nv_ref, a_ref, wd_ref, bd_ref, o_ref):
    i = pl.program_id(0)

    @pl.when(i < nv_ref[0])
    def _():
        _store_tiled(o_ref, jnp.dot(a_ref[...], wd_ref[0, 0], preferred_element_type=F32) + bd_ref[0, 0],
                     pitch=COMBINE_PITCH)

    @pl.when(i >= nv_ref[0])
    def _():
        o_ref[...] = jnp.zeros_like(o_ref)


def _moe_experts(x_sorted, tile_expert, n_valid, w_gu, b_gu, w_d, b_d, layer):
    f, d = w_d.shape[2], w_d.shape[3]
    s_per = d // LANES
    s_in = s_per // 2
    n_slots = x_sorted.shape[0] // s_in
    tm, tn = MOE_TM, MOE_TN
    n_tiles = n_slots // tm
    nj = f // tn
    act = pl.pallas_call(
        _moe_gu_kernel,
        grid_spec=pltpu.PrefetchScalarGridSpec(
            num_scalar_prefetch=2,
            grid=(nj, n_tiles),
            in_specs=[pl.BlockSpec((tm * s_in, LANES), lambda j, i, te, nv: (i, 0)),
                      pl.BlockSpec((1, 1, d, tn), lambda j, i, te, nv: (layer, te[i], 0, j)),
                      pl.BlockSpec((1, 1, d, tn), lambda j, i, te, nv: (layer, te[i], 0, nj + j)),
                      pl.BlockSpec((1, 1, 1, tn), lambda j, i, te, nv: (layer, te[i], 0, j)),
                      pl.BlockSpec((1, 1, 1, tn), lambda j, i, te, nv: (layer, te[i], 0, nj + j))],
            out_specs=pl.BlockSpec((tm, tn), lambda j, i, te, nv: (i, j)),
        ),
        out_shape=jax.ShapeDtypeStruct((n_slots, f), BF16),
        compiler_params=_cparams(("arbitrary", "arbitrary")),
        name="moe_gate_up",
    )(tile_expert, n_valid, x_sorted, w_gu, w_gu, b_gu, b_gu)
    y = pl.pallas_call(
        _moe_down_kernel,
        grid_spec=pltpu.PrefetchScalarGridSpec(
            num_scalar_prefetch=2,
            grid=(n_tiles,),
            in_specs=[pl.BlockSpec((tm, f), lambda i, te, nv: (i, 0)),
                      pl.BlockSpec((1, 1, f, d), lambda i, te, nv: (layer, te[i], 0, 0)),
                      pl.BlockSpec((1, 1, 1, d), lambda i, te, nv: (layer, te[i], 0, 0))],
            out_specs=pl.BlockSpec((tm * COMBINE_PITCH, LANES), lambda i, te, nv: (i, 0)),
        ),
        out_shape=jax.ShapeDtypeStruct((n_slots * COMBINE_PITCH, LANES), F32),
        compiler_params=_cparams(("arbitrary",)),
        name="moe_down",
    )(tile_expert, n_valid, act, w_d, b_d)
    return y


def _moe_plan(top_idx):
    t = top_idx.shape[0]
    n_items = t * MOE_TOPK
    n_slots = n_items + N_EXPERTS * MOE_TM
    items = top_idx.reshape(n_items)
    onehot = (items[:, None] == jnp.arange(N_EXPERTS, dtype=jnp.int32)[None, :]).astype(jnp.int32)
    csum = jnp.cumsum(onehot, axis=0)
    counts = csum[-1]
    rank = jnp.sum(csum * onehot, axis=1) - 1
    padded = (counts + MOE_TM - 1) // MOE_TM * MOE_TM
    pad_end = jnp.cumsum(padded)
    pad_start = pad_end - padded
    dest = (pad_start[items] + rank).astype(jnp.int32)
    tile_start = jnp.arange(n_slots // MOE_TM, dtype=jnp.int32) * MOE_TM
    tile_expert = jnp.minimum(jnp.searchsorted(pad_end, tile_start, side='right'), N_EXPERTS - 1).astype(jnp.int32)
    n_valid = (pad_end[-1] // MOE_TM).astype(jnp.int32).reshape(1)
    return dest, n_slots, tile_expert, n_valid


def _rope(x, cos_f, sin_s):
    lane = lax.broadcasted_iota(jnp.int32, x.shape, 1)
    half = ROPE_DIM // 2
    partner = jnp.where(lane < half, pltpu.roll(x, LANES - half, 1), pltpu.roll(x, half, 1))
    return x * cos_f + partner * sin_s


def _moba_kernel(q_ref, k_ref, v_ref, cos_ref, sin_ref, o_ref, *, n_blk):
    cos_f, sin_s = cos_ref[...], sin_ref[...]
    for hh in range(MOBA_HEADS_PER_STEP):
        cols = slice(hh * HEAD_DIM, (hh + 1) * HEAD_DIM)
        o_ref[:, cols] = _moba_head(q_ref[:, cols], k_ref[:, cols], v_ref[:, cols], cos_f, sin_s,
                                    n_blk).astype(o_ref.dtype)


def _moba_head(q, k, v, cos_f, sin_s, n_blk):
    blk = MOBA_BLOCK
    kr = _rope(k, cos_f, sin_s)
    means = [jnp.mean(kr[j * blk:(j + 1) * blk], axis=0, keepdims=True) for j in range(n_blk)]
    n_rows = max(n_blk, 8)
    if n_rows > n_blk:
        means.append(jnp.zeros((n_rows - n_blk, HEAD_DIM), F32))
    kmean = jnp.concatenate(means, axis=0).astype(BF16)
    ks = (kr * (HEAD_DIM ** -0.5)).astype(BF16)
    vt = v.T.astype(BF16)
    qt = _rope(q, cos_f, sin_s).T.astype(BF16)

    gate = jnp.dot(kmean, qt, preferred_element_type=F32)
    row_i = lax.broadcasted_iota(jnp.int32, gate.shape, 0)
    rowf = row_i.astype(F32)
    past = row_i < lax.broadcasted_iota(jnp.int32, gate.shape, 1) // blk
    cur = jnp.where(past, gate, NEG_INF)
    sel = jnp.zeros_like(gate)
    for _ in range(min(MOBA_TOPK, n_blk - 1)):
        m = jnp.max(cur, axis=0, keepdims=True)
        idx = jnp.min(jnp.where(cur == m, rowf, float(LANES)), axis=0, keepdims=True)
        hit = rowf == idx
        sel = jnp.where(hit & past, 1.0, sel)
        cur = jnp.where(hit, -3.0e38, cur)

    r_i = lax.broadcasted_iota(jnp.int32, (blk, blk), 0)
    c_i = lax.broadcasted_iota(jnp.int32, (blk, blk), 1)
    outs = []
    for qb in range(n_blk):
        cols = slice(qb * blk, (qb + 1) * blk)
        nk = (qb + 1) * blk
        s = jnp.dot(ks[:nk], qt[:, cols], preferred_element_type=F32)
        parts = [jnp.where(sel[j:j + 1, cols] > 0.0, s[j * blk:(j + 1) * blk], NEG_INF) for j in range(qb)]
        parts.append(jnp.where(r_i <= c_i, s[qb * blk:], NEG_INF))
        s = jnp.concatenate(parts, axis=0)
        p = jnp.exp(s - jnp.max(s, axis=0, keepdims=True))
        l = jnp.sum(p, axis=0, keepdims=True)
        acc = jnp.dot(vt[:, :nk], p.astype(BF16), preferred_element_type=F32)
        outs.append(acc / l)
    return jnp.concatenate(outs, axis=1).T


def _moba_attention(qkv, cos_f, sin_s, batch, seq):
    t = qkv.shape[0]
    n_blk = seq // MOBA_BLOCK
    h = MOBA_HEADS // MOBA_HEADS_PER_STEP
    head = lambda off: pl.BlockSpec((seq, MOBA_HEADS_PER_STEP * HEAD_DIM), lambda b, hh: (b, off + hh))
    table = pl.BlockSpec((seq, HEAD_DIM), lambda b, hh: (b, 0))
    return pl.pallas_call(
        functools.partial(_moba_kernel, n_blk=n_blk),
        grid=(batch, h),
        in_specs=[head(0), head(h), head(2 * h), table, table],
        out_specs=head(0),
        out_shape=jax.ShapeDtypeStruct((t, MOBA_HEADS * HEAD_DIM), BF16),
        compiler_params=_cparams(("parallel", "parallel")),
        name="moba_attention",
    )(qkv, qkv, qkv, cos_f, sin_s)


def _rope_tables(positions):
    half = ROPE_DIM // 2
    inv_freq = ROPE_THETA ** (-jnp.arange(0, ROPE_DIM, 2, dtype=F32) / ROPE_DIM)
    ang = positions.astype(F32).reshape(-1, 1) * inv_freq
    cos, sin = jnp.cos(ang), jnp.sin(ang)
    t = ang.shape[0]
    rest = HEAD_DIM - ROPE_DIM
    cos_f = jnp.concatenate([cos, cos, jnp.ones((t, rest), F32)], axis=1)
    sin_s = jnp.concatenate([-sin, sin, jnp.zeros((t, rest), F32)], axis=1)
    return cos_f, sin_s


def _conv_silu(x_ref, w):
    n = x_ref.shape[0]
    x = x_ref[...]
    row = lax.broadcasted_iota(jnp.int32, (8, x.shape[1]), 0)
    y = x * w[GDN_CONV - 1:GDN_CONV, :]
    for j in range(GDN_CONV - 1):
        k = GDN_CONV - 1 - j
        head = jnp.where(row >= k, pltpu.roll(x[0:8], k, 0), 0.0)
        shifted = jnp.concatenate([head, x_ref[pl.ds(8 - k, n - 8), :]], axis=0)
        y = y + shifted * w[j:j + 1, :]
    return y * jax.nn.sigmoid(y)


def _l2_normalize(y):
    return y * lax.rsqrt(jnp.sum(y * y, axis=1, keepdims=True) + L2_EPS)


def _gdn_gates_kernel(ba_ref, alog_ref, dtb_ref, gb_ref, gbt_ref):
    x = ba_ref[...]
    lane = lax.broadcasted_iota(jnp.int32, x.shape, 1)
    row = lax.broadcasted_iota(jnp.int32, x.shape, 0)
    xa = x + dtb_ref[...]
    softplus = jnp.maximum(xa, 0.0) + jnp.log(1.0 + jnp.exp(-jnp.abs(xa)))
    g = -jnp.exp(alog_ref[...]) * softplus
    pos = row % GDN_CHUNK
    sh = 1
    while sh < GDN_CHUNK:
        g = g + jnp.where(pos >= sh, pltpu.roll(g, sh, 0), 0.0)
        sh *= 2
    out = jnp.where(lane < GDN_V_HEADS, jax.nn.sigmoid(x), g)
    gb_ref[...] = out
    gbt_ref[...] = out.T


def _gdn_gates(ba, a_log, dt_bias, batch, seq):
    pad = jnp.zeros((LANES - 2 * GDN_V_HEADS,), F32)
    zeros = jnp.zeros((GDN_V_HEADS,), F32)
    alog_row = jnp.concatenate([zeros, a_log.astype(F32), pad]).reshape(1, LANES)
    dtb_row = jnp.concatenate([zeros, dt_bias.astype(F32), pad]).reshape(1, LANES)
    return pl.pallas_call(
        _gdn_gates_kernel,
        grid=(batch,),
        in_specs=[pl.BlockSpec((seq, LANES), lambda b: (b, 0)),
                  pl.BlockSpec((1, LANES), lambda b: (0, 0)),
                  pl.BlockSpec((1, LANES), lambda b: (0, 0))],
        out_specs=[pl.BlockSpec((seq, LANES), lambda b: (b, 0)),
                   pl.BlockSpec((LANES, seq), lambda b: (b, 0))],
        out_shape=[jax.ShapeDtypeStruct((batch * seq, LANES), F32),
                   jax.ShapeDtypeStruct((batch * LANES, seq), F32)],
        compiler_params=_cparams(("parallel",)),
        name="gdn_gates",
    )(ba, alog_row, dtb_row)


def _bmm(a, b):
    return jnp.einsum('nij,njk->nik', a.astype(BF16), b.astype(BF16), preferred_element_type=F32)


def _gdn_core_kernel(q_ref, k_ref, v_ref, z_ref, cwq_ref, cwk_ref, cwv_ref, gb_ref, gbt_ref, nw_ref, o_ref,
                     b_s, c_s, qe_s, o0_s, eg_s, st_s, *, n_chunks):
    c = GDN_CHUNK
    hk = pl.program_id(1)
    q2 = _l2_normalize(_conv_silu(q_ref, cwq_ref[...]))
    k2 = _l2_normalize(_conv_silu(k_ref, cwk_ref[...]))
    v2 = _conv_silu(v_ref, cwv_ref[...])
    q3 = (q2 * (HEAD_DIM ** -0.5)).reshape(n_chunks, c, HEAD_DIM)
    k3 = k2.reshape(n_chunks, c, HEAD_DIM)
    kt = k2.T
    qh, kh = q3.astype(BF16), k3.astype(BF16)
    kk = jnp.einsum('nid,njd->nij', kh, kh, preferred_element_type=F32)
    qk = jnp.einsum('nid,njd->nij', qh, kh, preferred_element_type=F32)
    r_i = lax.broadcasted_iota(jnp.int32, (c, c), 0)
    c_i = lax.broadcasted_iota(jnp.int32, (c, c), 1)
    tril = c_i <= r_i
    strict = c_i < r_i
    eye = (c_i == r_i).astype(F32)
    lane = lax.broadcasted_iota(jnp.int32, gb_ref.shape, 1)
    gb = gb_ref[...]

    for vh in range(2):
        hv = 2 * hk + vh
        beta = jnp.sum(jnp.where(lane == hv, gb, 0.0), axis=1, keepdims=True).reshape(n_chunks, c, 1)
        gcol = jnp.sum(jnp.where(lane == GDN_V_HEADS + hv, gb, 0.0), axis=1, keepdims=True).reshape(n_chunks, c, 1)
        grow = gbt_ref[pl.ds(GDN_V_HEADS + hv, 1), :]
        decay, kdt = [], []
        for n in range(n_chunks):
            gr = grow[:, n * c:(n + 1) * c]
            diff = gcol[n] - gr
            decay.append(jnp.exp(jnp.where(tril, diff, NEG_INF)))
            kdt.append(kt[:, n * c:(n + 1) * c] * jnp.exp(gr[:, c - 1:c] - gr))
        decay = jnp.stack(decay, axis=0)
        kdt = jnp.stack(kdt, axis=0)
        m = jnp.where(strict, beta * kk * decay, 0.0)
        attn = jnp.where(tril, qk * decay, 0.0)
        t = eye - jnp.where((r_i // 2 == c_i // 2), m, 0.0)
        mb = m.astype(BF16)
        s = 2
        while s < c:
            off = (r_i // (2 * s) == c_i // (2 * s)) & ((r_i % (2 * s)) >= s) & ((c_i % (2 * s)) < s)
            moff = jnp.where(off, mb, jnp.zeros_like(mb))
            tb = t.astype(BF16)
            t = t - _bmm(_bmm(tb, moff), tb)
            s *= 2
        eg = jnp.exp(gcol)
        v3 = v2[:, vh * HEAD_DIM:(vh + 1) * HEAD_DIM].reshape(n_chunks, c, HEAD_DIM)
        uw = _bmm(t, jnp.concatenate([v3 * beta, k3 * (beta * eg)], axis=-1))
        bc = _bmm(kdt, uw)
        ao = _bmm(attn, uw)
        b_s[vh] = bc[:, :, :HEAD_DIM]
        c_s[vh] = bc[:, :, HEAD_DIM:].astype(BF16)
        o0_s[vh] = ao[:, :, :HEAD_DIM]
        qe_s[vh] = (q3 * eg - ao[:, :, HEAD_DIM:]).astype(BF16)
        glast = gcol[:, c - 1:c, :]
        eg_s[vh] = jnp.broadcast_to(jnp.exp(glast), (n_chunks, 8, HEAD_DIM))
        st_s[vh] = jnp.zeros((HEAD_DIM, HEAD_DIM), F32)

    nw = nw_ref[...]

    def chunk_step(n, carry):
        rows = pl.ds(pl.multiple_of(n * c, c), c)
        for vh in range(2):
            state = st_s[vh]
            sb = state.astype(BF16)
            o = jnp.dot(qe_s[vh, n], sb, preferred_element_type=F32) + o0_s[vh, n]
            st_s[vh] = (state * eg_s[vh, n][0:1, :] + b_s[vh, n]
                        - jnp.dot(c_s[vh, n], sb, preferred_element_type=F32))
            zz = z_ref[rows, vh * HEAD_DIM:(vh + 1) * HEAD_DIM]
            on = o * lax.rsqrt(jnp.mean(o * o, axis=1, keepdims=True) + GDN_NORM_EPS)
            o_ref[rows, vh * HEAD_DIM:(vh + 1) * HEAD_DIM] = (on * nw * (zz * jax.nn.sigmoid(zz))).astype(o_ref.dtype)
        return carry

    lax.fori_loop(0, n_chunks, chunk_step, 0)


def _gdn_core(proj, conv_w, gb, gbt, norm_w, batch, seq):
    n_chunks = seq // GDN_CHUNK
    hd = HEAD_DIM
    vblk0 = 2 * GDN_KEY_DIM // (2 * hd)
    zblk0 = GDN_CONV_DIM // (2 * hd)
    big = lambda dt: pltpu.VMEM((2, n_chunks, GDN_CHUNK, hd), dt)
    return pl.pallas_call(
        functools.partial(_gdn_core_kernel, n_chunks=n_chunks),
        grid=(batch, GDN_K_HEADS),
        in_specs=[pl.BlockSpec((seq, hd), lambda b, h: (b, h)),
                  pl.BlockSpec((seq, hd), lambda b, h: (b, GDN_K_HEADS + h)),
                  pl.BlockSpec((seq, 2 * hd), lambda b, h: (b, vblk0 + h)),
                  pl.BlockSpec((seq, 2 * hd), lambda b, h: (b, zblk0 + h)),
                  pl.BlockSpec((GDN_CONV, hd), lambda b, h: (0, h)),
                  pl.BlockSpec((GDN_CONV, hd), lambda b, h: (0, GDN_K_HEADS + h)),
                  pl.BlockSpec((GDN_CONV, 2 * hd), lambda b, h: (0, vblk0 + h)),
                  pl.BlockSpec((seq, LANES), lambda b, h: (b, 0)),
                  pl.BlockSpec((LANES, seq), lambda b, h: (b, 0)),
                  pl.BlockSpec((1, hd), lambda b, h: (0, 0))],
        out_specs=pl.BlockSpec((seq, 2 * hd), lambda b, h: (b, h)),
        out_shape=jax.ShapeDtypeStruct((batch * seq, GDN_VALUE_DIM), BF16),
        scratch_shapes=[big(F32), big(BF16), big(BF16), big(F32),
                        pltpu.VMEM((2, n_chunks, 8, hd), F32), pltpu.VMEM((2, hd, hd), F32)],
        compiler_params=_cparams(("parallel", "arbitrary")),
        name="gdn_core",
    )(proj, proj, proj, proj, conv_w, conv_w, conv_w, gb, gbt, norm_w.reshape(1, hd).astype(F32))


def _gated_deltanet(h, w_in, conv_w, a_log, dt_bias, norm_w, w_out, batch, seq):
    w_main = w_in[:, :GDN_MAIN_DIM].astype(BF16)
    w_ba = jnp.pad(w_in[:, GDN_MAIN_DIM:], ((0, 0), (0, LANES - 2 * GDN_V_HEADS))).astype(BF16)
    proj = _matmul(h, w_main, out_dtype=F32, tm=1024, tn=1024, name="gdn_in_proj")
    ba = _matmul(h, w_ba, out_dtype=F32, tm=1024, tn=LANES, name="gdn_ba_proj")
    gb, gbt = _gdn_gates(ba, a_log, dt_bias, batch, seq)
    o = _gdn_core(proj, conv_w, gb, gbt, norm_w, batch, seq)
    return _matmul(o, w_out.astype(BF16), out_dtype=F32, tm=512, tn=1024, name="gdn_out_proj")


def _adaln(c, w, b):
    m = _matmul(c, w.astype(BF16), out_dtype=F32, tm=c.shape[0], tn=1024, bias=b, silu_a=True, name="adaln")
    d = w.shape[0]
    return m[:, :d], m[:, d:2 * d], m[:, 2 * d:]


def _moe_sublayer(x, h, top_idx, gate_w, gate_c, lng, lnb, seq, moe_w, layer, nxt):
    dest, n_slots, tile_expert, n_valid = _moe_plan(top_idx[:, :MOE_TOPK])
    x_sorted = _dispatch_rows(h, dest, n_slots)
    y_sorted = _moe_experts(x_sorted, tile_expert, n_valid, *moe_w, layer)
    return _post(x, y_sorted, gate_c, lng, lnb, seq, w4=(gate_w, dest), nxt=nxt)


def kernel(x, c, positions, ada_w, ada_b, ln_g, ln_b, moba_w_qkv, moba_w_o, gdn_w_in, gdn_conv_w, gdn_a_log,
           gdn_dt_bias, gdn_norm_w, gdn_w_out, router_w, router_b, moe_w_gate_up, moe_b_gate_up, moe_w_down,
           moe_b_down):
    batch, seq, d = x.shape
    t = batch * seq
    xs = x.reshape(t, d)
    cos_f, sin_s = _rope_tables(positions)
    mods = [[_adaln(c, ada_w[i, j], ada_b[i, j]) for j in range(2)] for i in range(DEPTH)]
    moe_w = (moe_w_gate_up.astype(BF16), moe_b_gate_up[:, :, None, :], moe_w_down.astype(BF16),
             moe_b_down[:, :, None, :])

    def router(i):
        wr = jnp.pad(router_w[i], ((0, 0), (0, LANES - N_EXPERTS)))
        br = jnp.concatenate([router_b[i], jnp.full((LANES - N_EXPERTS,), NEG_INF, F32)]).reshape(1, LANES)
        return wr, br

    for i in range(DEPTH):
        shift, scale, gate = mods[i][0]
        if i == 0:
            h = _modulate(xs, scale, shift, seq, BF16)
        if i % 2 == 0:
            qkv = _matmul(h, moba_w_qkv[i // 2].astype(BF16), out_dtype=F32, tm=1024, tn=1024, name="moba_qkv")
            o = _moba_attention(qkv, cos_f, sin_s, batch, seq)
            y = _matmul(o, moba_w_o[i // 2].astype(BF16), out_dtype=F32, tm=1024, tn=1024, name="moba_out")
        else:
            j = i // 2
            y = _gated_deltanet(h, gdn_w_in[j], gdn_conv_w[j], gdn_a_log[j], gdn_dt_bias[j], gdn_norm_w[j],
                                gdn_w_out[j], batch, seq)
        shift2, scale2, gate2 = mods[i][1]
        xs, h32, top_idx, gate_w = _post(xs, y, gate, ln_g[i, 0], ln_b[i, 0], seq, nxt=(scale2, shift2),
                                         router=router(i), h_tiled=True)
        nxt = None
        if i + 1 < DEPTH:
            shift_n, scale_n, _ = mods[i + 1][0]
            nxt = (scale_n, shift_n)
        res = _moe_sublayer(xs, h32, top_idx, gate_w, gate2, ln_g[i, 1], ln_b[i, 1], seq, moe_w, i, nxt)
        xs = res[0]
        if nxt is not None:
            h = res[1]
    return xs.reshape(batch, seq, d)
```
